```python
import math
import jax, jax.numpy as jnp
from jax import lax
import numpy as np

D_MODEL = 1024
BATCH = 2
SEQ = 8192
DEPTH = 2

N_A_LAYERS = DEPTH // 2
N_B_LAYERS = DEPTH - N_A_LAYERS

CONV_WIDTH = D_MODEL
CONV_K = 3

HEAD_DIM = 64
N_Q_HEADS = D_MODEL // HEAD_DIM
N_KV_HEADS = max(1, N_Q_HEADS // 8)
GROUP = N_Q_HEADS // N_KV_HEADS
ATTN_WIDTH = N_Q_HEADS * HEAD_DIM
KV_WIDTH = N_KV_HEADS * HEAD_DIM
WINDOW = 128
BLOCK = 128

N_BUCKETS = 32
MAX_DISTANCE = 128

EPS = 1e-6
NEG_INF = -1e30

kernel_name = "yoco_shortconv_swa_sink_hybrid"


def rmsnorm(x, g):
    xf = x.astype(jnp.float32)
    y = xf * lax.rsqrt(jnp.mean(xf * xf, axis=-1, keepdims=True) + EPS) * g.astype(jnp.float32)
    return y.astype(x.dtype)


def t5_causal_bucket(dist):
    max_exact = N_BUCKETS // 2
    is_small = dist < max_exact
    d = jnp.maximum(dist, 1).astype(jnp.float32)
    large = max_exact + (jnp.log(d / max_exact) / math.log(MAX_DISTANCE / max_exact)
                         * (N_BUCKETS - max_exact)).astype(jnp.int32)
    large = jnp.minimum(large, N_BUCKETS - 1)
    return jnp.where(is_small, dist, large)


def short_conv_mixer(h, w_in, conv_w, w_out):
    proj = h @ w_in
    b_gate, c_gate, u, z = jnp.split(proj, 4, axis=-1)
    v = c_gate * u
    conv = lax.conv_general_dilated(
        v, conv_w[:, None, :].astype(v.dtype),
        window_strides=(1,), padding=[(CONV_K - 1, 0)],
        dimension_numbers=("NWC", "WIO", "NWC"),
        feature_group_count=CONV_WIDTH)
    y = b_gate * conv * jax.nn.silu(z)
    return y @ w_out


def shared_kv(h, kv_norm, w_kv):
    bsz, seq, _ = h.shape
    nb = seq // BLOCK
    kv = rmsnorm(h, kv_norm) @ w_kv
    k, v = jnp.split(kv, 2, axis=-1)
    k = k.reshape(bsz, nb, BLOCK, N_KV_HEADS, HEAD_DIM)
    v = v.reshape(bsz, nb, BLOCK, N_KV_HEADS, HEAD_DIM)

    def band(t):
        prev = jnp.concatenate([jnp.zeros_like(t[:, :1]), t[:, :-1]], axis=1)
        return jnp.concatenate([prev, t], axis=2)

    return band(k), band(v)


def banded_bias_and_mask(nb, rel_bias):
    q_loc = jnp.arange(BLOCK, dtype=jnp.int32)[:, None]
    s_loc = jnp.arange(2 * BLOCK, dtype=jnp.int32)[None, :]
    dist = q_loc + BLOCK - s_loc
    in_window = (dist >= 0) & (dist < WINDOW)
    bucket = t5_causal_bucket(jnp.maximum(dist, 0))
    bias = rel_bias.astype(jnp.float32)[bucket]
    bias = jnp.transpose(bias, (2, 0, 1)).reshape(N_KV_HEADS, GROUP, BLOCK, 2 * BLOCK)
    blk = jnp.arange(nb, dtype=jnp.int32)[:, None, None]
    exists = (blk > 0) | (s_loc >= BLOCK)[None]
    mask = in_window[None] & exists
    return bias, mask


def swa_sink_attention(q, keys, vals, sinks, bias, mask):
    bsz, seq, _ = q.shape
    nb = seq // BLOCK
    qb = q.reshape(bsz, nb, BLOCK, N_KV_HEADS, GROUP, HEAD_DIM)
    scores = jnp.einsum("bnqkgd,bnskd->bnkgqs", qb, keys).astype(jnp.float32)
    logits = scores * (HEAD_DIM ** -0.5) + bias[None, None]
    logits = jnp.where(mask[None, :, None, None], logits, NEG_INF)
    sink = sinks.astype(jnp.float32).reshape(1, 1, N_KV_HEADS, GROUP, 1, 1)
    m = jnp.maximum(jnp.max(logits, axis=-1, keepdims=True), sink)
    p = jnp.exp(logits - m)
    p = p / (jnp.sum(p, axis=-1, keepdims=True) + jnp.exp(sink - m))
    out = jnp.einsum("bnkgqs,bnskd->bnqkgd", p.astype(vals.dtype), vals)
    return out.reshape(bsz, seq, ATTN_WIDTH)


def setup_inputs(seed: int = 0) -> dict:
    key = jax.random.key(seed)
    ks = jax.random.split(key, 16)
    f32 = jnp.float32
    nrm = lambda k, shape, s: jax.random.normal(k, shape, f32) * s
    return {
        "x": nrm(ks[0], (BATCH, SEQ, D_MODEL), 1.0),
        "a_pre_norm": 1.0 + nrm(ks[1], (N_A_LAYERS, D_MODEL), 0.05),
        "a_w_in": nrm(ks[2], (N_A_LAYERS, D_MODEL, 4 * CONV_WIDTH), D_MODEL ** -0.5),
        "a_conv_w": nrm(ks[3], (N_A_LAYERS, CONV_K, CONV_WIDTH), CONV_K ** -0.5),
        "a_w_out": nrm(ks[4], (N_A_LAYERS, CONV_WIDTH, D_MODEL), CONV_WIDTH ** -0.5),
        "a_post_norm": 1.0 + nrm(ks[5], (N_A_LAYERS, D_MODEL), 0.05),
        "kv_norm": 1.0 + nrm(ks[6], (D_MODEL,), 0.05),
        "w_kv": nrm(ks[7], (D_MODEL, 2 * KV_WIDTH), D_MODEL ** -0.5),
        "rel_bias": nrm(ks[8], (N_BUCKETS, N_Q_HEADS), 0.1),
        "b_pre_norm": 1.0 + nrm(ks[9], (N_B_LAYERS, D_MODEL), 0.05),
        "b_w_in": nrm(ks[10], (N_B_LAYERS, D_MODEL, 2 * ATTN_WIDTH), D_MODEL ** -0.5),
        "b_sinks": nrm(ks[11], (N_B_LAYERS, N_Q_HEADS), 0.5),
        "b_w_out": nrm(ks[12], (N_B_LAYERS, ATTN_WIDTH, D_MODEL), ATTN_WIDTH ** -0.5),
        "b_post_norm": 1.0 + nrm(ks[13], (N_B_LAYERS, D_MODEL), 0.05),
    }


def reference(x, a_pre_norm, a_w_in, a_conv_w, a_w_out, a_post_norm,
              kv_norm, w_kv, rel_bias,
              b_pre_norm, b_w_in, b_sinks, b_w_out, b_post_norm):
    h = x
    nb = x.shape[1] // BLOCK
    bias, mask = banded_bias_and_mask(nb, rel_bias)
    keys = vals = None
    for layer in range(DEPTH):
        if layer < N_A_LAYERS:
            i = layer
            y = short_conv_mixer(rmsnorm(h, a_pre_norm[i]), a_w_in[i], a_conv_w[i], a_w_out[i])
            h = h + rmsnorm(y, a_post_norm[i])
            if layer == N_A_LAYERS - 1:
                keys, vals = shared_kv(h, kv_norm, w_kv)
        else:
            j = layer - N_A_LAYERS
            qz = rmsnorm(h, b_pre_norm[j]) @ b_w_in[j]
            q, z = jnp.split(qz, 2, axis=-1)
            o = swa_sink_attention(q, keys, vals, b_sinks[j], bias, mask) * jax.nn.silu(z)
            y = o @ b_w_out[j]
            h = h + rmsnorm(y, b_post_norm[j])
    return h
```

```python
import math

import jax
import jax.numpy as jnp
import numpy as np
from jax.experimental import pallas as pl
from jax.experimental.pallas import tpu as pltpu

D_MODEL = 1024
CONV_K = 3
HEAD_DIM = 64
N_Q_HEADS = D_MODEL // HEAD_DIM
N_KV_HEADS = 2
GROUP = N_Q_HEADS // N_KV_HEADS
KV_WIDTH = N_KV_HEADS * HEAD_DIM
BLOCK = 128
N_BUCKETS = 32
MAX_DISTANCE = 128
EPS = 1e-6
NEG_INF = -1e30

LANES = 128
SUBLANES = 8
HEADS_PER_VREG = LANES // HEAD_DIM
N_PAIRS = N_Q_HEADS // HEADS_PER_VREG
PAIRS_PER_KV = GROUP // HEADS_PER_VREG
ROW_TILE = 256
COL_CHUNK = 256
VMEM_LIMIT_BYTES = 52 * 1024 * 1024


def _bucket_table():
    q_loc = np.arange(BLOCK, dtype=np.int32)[:, None]
    s_loc = np.arange(2 * BLOCK, dtype=np.int32)[None, :]
    dist = q_loc + BLOCK - s_loc
    in_window = (dist >= 0) & (dist < BLOCK)
    d = np.maximum(dist, 0)
    max_exact = N_BUCKETS // 2
    df = np.maximum(d, 1).astype(np.float32)
    large = max_exact + (
        np.log(df / np.float32(max_exact)) / np.float32(math.log(MAX_DISTANCE / max_exact))
        * np.float32(N_BUCKETS - max_exact)).astype(np.int32)
    large = np.minimum(large, N_BUCKETS - 1)
    bucket = np.where(d < max_exact, d, large)
    return np.where(in_window, bucket, -1).astype(np.int32)


def _rms_scale(x):
    return x * jax.lax.rsqrt(jnp.mean(x * x, axis=-1, keepdims=True) + EPS)


def _trunk_kernel(rel_bias_ref, sinks_ref, bucket_ref, x_ref,
                  a_pre_ref, w_in_ref, conv_w_ref, w_out_a_ref, a_post_ref,
                  kv_norm_ref, w_kv_ref, b_pre_ref, w_in_b_ref, w_out_b_ref, b_post_ref,
                  out_ref,
                  bias_scr, hn_scr, v_scr, y_scr, q_scr, z_scr, kz_scr, vz_scr, g_scr):
    batch = pl.program_id(0)
    tile = pl.program_id(1)
    tm = x_ref.shape[0]
    f32 = jnp.float32
    bf16 = jnp.bfloat16

    @pl.when((batch == 0) & (tile == 0))
    def _build_bias():
        bucket = bucket_ref[...]
        has_prev = jax.lax.broadcasted_iota(jnp.int32, bucket.shape, 1) >= BLOCK
        for h in range(N_Q_HEADS):
            acc = jnp.full(bucket.shape, NEG_INF, f32)
            for b in range(N_BUCKETS):
                acc = jnp.where(bucket == b, rel_bias_ref[b, h], acc)
            bias_scr[0, h] = acc
            bias_scr[1, h] = jnp.where(has_prev, acc, NEG_INF)

    @pl.when(tile == 0)
    def _reset_carry():
        v_scr[0:SUBLANES, :] = jnp.zeros((SUBLANES, D_MODEL), f32)
        kz_scr[:, 0:BLOCK, :] = jnp.zeros((2 * N_KV_HEADS, BLOCK, LANES), bf16)
        vz_scr[:, 0:BLOCK, :] = jnp.zeros((2 * N_KV_HEADS, BLOCK, LANES), bf16)

    @pl.when(tile > 0)
    def _shift_carry():
        v_scr[0:SUBLANES, :] = v_scr[tm:tm + SUBLANES, :]
        kz_scr[:, 0:BLOCK, :] = kz_scr[:, tm:tm + BLOCK, :]
        vz_scr[:, 0:BLOCK, :] = vz_scr[:, tm:tm + BLOCK, :]

    x = x_ref[...]
    hn_scr[...] = (_rms_scale(x) * a_pre_ref[...]).astype(bf16)
    width = D_MODEL
    for c0 in range(0, width, COL_CHUNK):
        cols = slice(c0, c0 + COL_CHUNK)
        hn = hn_scr[...]
        b_gate = jnp.dot(hn, w_in_ref[:, c0:c0 + COL_CHUNK], preferred_element_type=f32)
        c_gate = jnp.dot(hn, w_in_ref[:, width + c0:width + c0 + COL_CHUNK],
                         preferred_element_type=f32)
        u = jnp.dot(hn, w_in_ref[:, 2 * width + c0:2 * width + c0 + COL_CHUNK],
                    preferred_element_type=f32)
        z = jnp.dot(hn, w_in_ref[:, 3 * width + c0:3 * width + c0 + COL_CHUNK],
                    preferred_element_type=f32)
        v_scr[SUBLANES:SUBLANES + tm, cols] = c_gate * u
        conv = (conv_w_ref[0:1, cols] * v_scr[SUBLANES - 2:SUBLANES - 2 + tm, cols]
                + conv_w_ref[1:2, cols] * v_scr[SUBLANES - 1:SUBLANES - 1 + tm, cols]
                + conv_w_ref[2:3, cols] * v_scr[SUBLANES:SUBLANES + tm, cols])
        y_scr[:, cols] = (b_gate * conv * (z * jax.nn.sigmoid(z))).astype(bf16)
    y = jnp.dot(y_scr[...], w_out_a_ref[...], preferred_element_type=f32)
    h1 = x + _rms_scale(y) * a_post_ref[...]

    h1n = _rms_scale(h1)
    kv = jnp.dot((h1n * kv_norm_ref[...]).astype(bf16), w_kv_ref[...],
                 preferred_element_type=f32)
    low_half = jax.lax.broadcasted_iota(jnp.int32, (tm, LANES), 1) < HEAD_DIM
    for src, dst in ((kv[:, 0:KV_WIDTH], kz_scr), (kv[:, KV_WIDTH:2 * KV_WIDTH], vz_scr)):
        swapped = pltpu.roll(src, HEAD_DIM, axis=1)
        dst[0, BLOCK:BLOCK + tm, :] = jnp.where(low_half, src, 0.0).astype(bf16)
        dst[1, BLOCK:BLOCK + tm, :] = jnp.where(low_half, 0.0, swapped).astype(bf16)
        dst[2, BLOCK:BLOCK + tm, :] = jnp.where(low_half, swapped, 0.0).astype(bf16)
        dst[3, BLOCK:BLOCK + tm, :] = jnp.where(low_half, 0.0, src).astype(bf16)

    qn = (h1n * b_pre_ref[...]).astype(bf16)
    q = jnp.dot(qn, w_in_b_ref[:, 0:D_MODEL], preferred_element_type=f32)
    q_scr[...] = (q * (HEAD_DIM ** -0.5)).astype(bf16)
    z_scr[...] = jnp.dot(qn, w_in_b_ref[:, D_MODEL:2 * D_MODEL], preferred_element_type=f32)

    for j in range(tm // BLOCK):
        rows = slice(j * BLOCK, (j + 1) * BLOCK)
        key_rows = slice(j * BLOCK, (j + 2) * BLOCK)
        first_block = jnp.where((tile == 0) & (j == 0), 1, 0) if j == 0 else 0
        for pair in range(N_PAIRS):
            lanes = slice(pair * LANES, (pair + 1) * LANES)
            q_pair = q_scr[rows, lanes]
            o_pair = None
            for parity in range(HEADS_PER_VREG):
                head = pair * HEADS_PER_VREG + parity
                slot = (pair // PAIRS_PER_KV) * HEADS_PER_VREG + parity
                s = jax.lax.dot_general(q_pair, kz_scr[slot, key_rows, :],
                                        (((1,), (1,)), ((), ())),
                                        preferred_element_type=f32)
                s = s + bias_scr[first_block, head]
                sink = sinks_ref[0, head]
                m = jnp.maximum(jnp.max(s, axis=-1, keepdims=True), sink)
                p = jnp.exp(s - m)
                denom = jnp.sum(p, axis=-1, keepdims=True) + jnp.exp(sink - m)
                pv = jnp.dot(p.astype(bf16), vz_scr[slot, key_rows, :],
                             preferred_element_type=f32)
                pv = pv * (1.0 / denom)
                o_pair = pv if o_pair is None else o_pair + pv
            zg = z_scr[rows, lanes]
            g_scr[rows, lanes] = (o_pair * (zg * jax.nn.sigmoid(zg))).astype(bf16)

    y2 = jnp.dot(g_scr[...], w_out_b_ref[...], preferred_element_type=f32)
    out_ref[...] = h1 + _rms_scale(y2) * b_post_ref[...]


def _resident(shape):
    return pl.BlockSpec(shape, lambda b, i: (0,) * len(shape), pipeline_mode=pl.Buffered(1))


@jax.jit
def kernel(x, a_pre_norm, a_w_in, a_conv_w, a_w_out, a_post_norm, kv_norm, w_kv, rel_bias,
           b_pre_norm, b_w_in, b_sinks, b_w_out, b_post_norm):
    bsz, seq, d_model = x.shape
    assert d_model == D_MODEL and seq % ROW_TILE == 0
    assert a_w_in.shape[0] == 1 and b_w_in.shape[0] == 1
    tm = ROW_TILE
    bf16 = jnp.bfloat16
    f32 = jnp.float32
    row = lambda g: g.reshape(1, D_MODEL).astype(f32)
    smem = pl.BlockSpec(memory_space=pltpu.SMEM)

    in_specs = [
        smem,
        smem,
        _resident((BLOCK, 2 * BLOCK)),
        pl.BlockSpec((None, tm, D_MODEL), lambda b, i: (b, i, 0)),
        _resident((1, D_MODEL)),
        _resident((D_MODEL, 4 * D_MODEL)),
        _resident((CONV_K, D_MODEL)),
        _resident((D_MODEL, D_MODEL)),
        _resident((1, D_MODEL)),
        _resident((1, D_MODEL)),
        _resident((D_MODEL, 2 * KV_WIDTH)),
        _resident((1, D_MODEL)),
        _resident((D_MODEL, 2 * D_MODEL)),
        _resident((D_MODEL, D_MODEL)),
        _resident((1, D_MODEL)),
    ]
    scratch_shapes = [
        pltpu.VMEM((2, N_Q_HEADS, BLOCK, 2 * BLOCK), f32),
        pltpu.VMEM((tm, D_MODEL), bf16),
        pltpu.VMEM((tm + SUBLANES, D_MODEL), f32),
        pltpu.VMEM((tm, D_MODEL), bf16),
        pltpu.VMEM((tm, D_MODEL), bf16),
        pltpu.VMEM((tm, D_MODEL), f32),
        pltpu.VMEM((2 * N_KV_HEADS, tm + BLOCK, LANES), bf16),
        pltpu.VMEM((2 * N_KV_HEADS, tm + BLOCK, LANES), bf16),
        pltpu.VMEM((tm, D_MODEL), bf16),
    ]
    return pl.pallas_call(
        _trunk_kernel,
        grid=(bsz, seq // tm),
        in_specs=in_specs,
        out_specs=pl.BlockSpec((None, tm, D_MODEL), lambda b, i: (b, i, 0)),
        out_shape=jax.ShapeDtypeStruct(x.shape, f32),
        scratch_shapes=scratch_shapes,
        compiler_params=pltpu.CompilerParams(
            dimension_semantics=("arbitrary", "arbitrary"),
            vmem_limit_bytes=VMEM_LIMIT_BYTES),
        name="yoco_trunk",
    )(
        rel_bias.astype(f32), b_sinks.reshape(1, N_Q_HEADS).astype(f32),
        jnp.asarray(_bucket_table()), x,
        row(a_pre_norm), a_w_in[0].astype(bf16), a_conv_w[0].astype(f32),
        a_w_out[0].astype(bf16), row(a_post_norm),
        row(kv_norm), w_kv.astype(bf16), row(b_pre_norm), b_w_in[0].astype(bf16),
        b_w_out[0].astype(bf16), row(b_post_norm),
    )
```

```python
import functools
import math

import jax
import jax.numpy as jnp
import numpy as np
from jax.experimental import pallas as pl
from jax.experimental.pallas import tpu as pltpu

D_MODEL = 1024
CONV_K = 3
HEAD_DIM = 64
N_Q_HEADS = D_MODEL // HEAD_DIM
N_KV_HEADS = 2
GROUP = N_Q_HEADS // N_KV_HEADS
KV_WIDTH = N_KV_HEADS * HEAD_DIM
BLOCK = 128
N_BUCKETS = 32
MAX_DISTANCE = 128
EPS = 1e-6
NEG_INF = -1e30

LANES = 128
SUBLANES = 8
HEADS_PER_VREG = LANES // HEAD_DIM
N_PAIRS = N_Q_HEADS // HEADS_PER_VREG
PAIRS_PER_KV = GROUP // HEADS_PER_VREG
ROW_TILE = 256
COL_CHUNK = 256
VMEM_LIMIT_BYTES = 52 * 1024 * 1024


def _bucket_table():
    q_loc = np.arange(BLOCK, dtype=np.int32)[:, None]
    s_loc = np.arange(2 * BLOCK, dtype=np.int32)[None, :]
    dist = q_loc + BLOCK - s_loc
    in_window = (dist >= 0) & (dist < BLOCK)
    d = np.maximum(dist, 0)
    max_exact = N_BUCKETS // 2
    df = np.maximum(d, 1).astype(np.float32)
    large = max_exact + (
        np.log(df / np.float32(max_exact)) / np.float32(math.log(MAX_DISTANCE / max_exact))
        * np.float32(N_BUCKETS - max_exact)).astype(np.int32)
    large = np.minimum(large, N_BUCKETS - 1)
    bucket = np.where(d < max_exact, d, large)
    return np.where(in_window, bucket, -1).astype(np.int32)


def _rms_scale(x):
    return x * jax.lax.rsqrt(jnp.mean(x * x, axis=-1, keepdims=True) + EPS)


def _trunk_kernel(rel_bias_ref, sinks_ref, bucket_ref, x_ref,
                  a_pre_ref, w_in_ref, conv_w_ref, w_out_a_ref, a_post_ref,
                  kv_norm_ref, w_kv_ref, b_pre_ref, w_in_b_ref, w_out_b_ref, b_post_ref,
                  out_ref,
                  bias_scr, hn_scr, v_scr, y_scr, h1_scr, q_scr, z_scr, kz_scr, vz_scr, g_scr,
                  *, tiles_per_seq):
    step = pl.program_id(0)
    tm = x_ref.shape[0]
    f32 = jnp.float32
    bf16 = jnp.bfloat16

    @pl.when(step == 0)
    def _init():
        bucket = bucket_ref[...]
        has_prev = jax.lax.broadcasted_iota(jnp.int32, bucket.shape, 1) >= BLOCK
        for h in range(N_Q_HEADS):
            acc = jnp.full(bucket.shape, NEG_INF, f32)
            for b in range(N_BUCKETS):
                acc = jnp.where(bucket == b, rel_bias_ref[b, h], acc)
            bias_scr[0, h] = acc
            bias_scr[1, h] = jnp.where(has_prev, acc, NEG_INF)
        v_scr[...] = jnp.zeros(v_scr.shape, f32)
        h1_scr[...] = jnp.zeros(h1_scr.shape, f32)
        q_scr[...] = jnp.zeros(q_scr.shape, bf16)
        z_scr[...] = jnp.zeros(z_scr.shape, f32)
        kz_scr[...] = jnp.zeros(kz_scr.shape, bf16)
        vz_scr[...] = jnp.zeros(vz_scr.shape, bf16)

    b_seq_start = jax.lax.rem(step + (tiles_per_seq - 1), tiles_per_seq) == 0
    out_ref[...] = h1_scr[...]
    for j in range(tm // BLOCK):
        rows = slice(j * BLOCK, (j + 1) * BLOCK)
        key_rows = slice(j * BLOCK, (j + 2) * BLOCK)
        first_block = jnp.where(b_seq_start, 1, 0) if j == 0 else 0
        for pair in range(N_PAIRS):
            lanes = slice(pair * LANES, (pair + 1) * LANES)
            q_pair = q_scr[rows, lanes]
            o_pair = None
            for parity in range(HEADS_PER_VREG):
                head = pair * HEADS_PER_VREG + parity
                slot = (pair // PAIRS_PER_KV) * HEADS_PER_VREG + parity
                s = jax.lax.dot_general(q_pair, kz_scr[slot, key_rows, :],
                                        (((1,), (1,)), ((), ())),
                                        preferred_element_type=f32)
                s = s + bias_scr[first_block, head]
                sink = sinks_ref[0, head]
                m = jnp.maximum(jnp.max(s, axis=-1, keepdims=True), sink)
                p = jnp.exp(s - m)
                denom = jnp.sum(p, axis=-1, keepdims=True) + jnp.exp(sink - m)
                pv = jnp.dot(p.astype(bf16), vz_scr[slot, key_rows, :],
                             preferred_element_type=f32)
                pv = pv * (1.0 / denom)
                o_pair = pv if o_pair is None else o_pair + pv
            zg = z_scr[rows, lanes]
            g_scr[rows, lanes] = (o_pair * (zg * jax.nn.sigmoid(zg))).astype(bf16)
    y2 = jnp.dot(g_scr[...], w_out_b_ref[...], preferred_element_type=f32)
    out_ref[...] = out_ref[...] + _rms_scale(y2) * b_post_ref[...]

    a_seq_start = jax.lax.rem(step, tiles_per_seq) == 0
    v_scr[0:SUBLANES, :] = jnp.where(a_seq_start, 0.0, v_scr[tm:tm + SUBLANES, :])
    kz_scr[:, 0:BLOCK, :] = jnp.where(
        a_seq_start, 0.0, kz_scr[:, tm:tm + BLOCK, :].astype(f32)).astype(bf16)
    vz_scr[:, 0:BLOCK, :] = jnp.where(
        a_seq_start, 0.0, vz_scr[:, tm:tm + BLOCK, :].astype(f32)).astype(bf16)

    x = x_ref[...]
    hn_scr[...] = (_rms_scale(x) * a_pre_ref[...]).astype(bf16)
    width = D_MODEL
    for c0 in range(0, width, COL_CHUNK):
        cols = slice(c0, c0 + COL_CHUNK)
        hn = hn_scr[...]
        b_gate = jnp.dot(hn, w_in_ref[:, c0:c0 + COL_CHUNK], preferred_element_type=f32)
        c_gate = jnp.dot(hn, w_in_ref[:, width + c0:width + c0 + COL_CHUNK],
                         preferred_element_type=f32)
        u = jnp.dot(hn, w_in_ref[:, 2 * width + c0:2 * width + c0 + COL_CHUNK],
                    preferred_element_type=f32)
        z = jnp.dot(hn, w_in_ref[:, 3 * width + c0:3 * width + c0 + COL_CHUNK],
                    preferred_element_type=f32)
        v_scr[SUBLANES:SUBLANES + tm, cols] = c_gate * u
        conv = (conv_w_ref[0:1, cols] * v_scr[SUBLANES - 2:SUBLANES - 2 + tm, cols]
                + conv_w_ref[1:2, cols] * v_scr[SUBLANES - 1:SUBLANES - 1 + tm, cols]
                + conv_w_ref[2:3, cols] * v_scr[SUBLANES:SUBLANES + tm, cols])
        y_scr[:, cols] = (b_gate * conv * (z * jax.nn.sigmoid(z))).astype(bf16)
    y = jnp.dot(y_scr[...], w_out_a_ref[...], preferred_element_type=f32)
    h1 = x + _rms_scale(y) * a_post_ref[...]
    h1_scr[...] = h1

    h1n = _rms_scale(h1)
    kv = jnp.dot((h1n * kv_norm_ref[...]).astype(bf16), w_kv_ref[...],
                 preferred_element_type=f32)
    low_half = jax.lax.broadcasted_iota(jnp.int32, (tm, LANES), 1) < HEAD_DIM
    for src, dst in ((kv[:, 0:KV_WIDTH], kz_scr), (kv[:, KV_WIDTH:2 * KV_WIDTH], vz_scr)):
        swapped = pltpu.roll(src, HEAD_DIM, axis=1)
        dst[0, BLOCK:BLOCK + tm, :] = jnp.where(low_half, src, 0.0).astype(bf16)
        dst[1, BLOCK:BLOCK + tm, :] = jnp.where(low_half, 0.0, swapped).astype(bf16)
        dst[2, BLOCK:BLOCK + tm, :] = jnp.where(low_half, swapped, 0.0).astype(bf16)
        dst[3, BLOCK:BLOCK + tm, :] = jnp.where(low_half, 0.0, src).astype(bf16)

    qn = (h1n * b_pre_ref[...]).astype(bf16)
    q = jnp.dot(qn, w_in_b_ref[:, 0:D_MODEL], preferred_element_type=f32)
    q_scr[...] = (q * (HEAD_DIM ** -0.5)).astype(bf16)
    z_scr[...] = jnp.dot(qn, w_in_b_ref[:, D_MODEL:2 * D_MODEL], preferred_element_type=f32)


def _resident(shape):
    return pl.BlockSpec(shape, lambda s: (0,) * len(shape), pipeline_mode=pl.Buffered(1))


@jax.jit
def kernel(x, a_pre_norm, a_w_in, a_conv_w, a_w_out, a_post_norm, kv_norm, w_kv, rel_bias,
           b_pre_norm, b_w_in, b_sinks, b_w_out, b_post_norm):
    bsz, seq, d_model = x.shape
    assert d_model == D_MODEL and seq % ROW_TILE == 0
    assert a_w_in.shape[0] == 1 and b_w_in.shape[0] == 1
    tm = ROW_TILE
    tiles_per_seq = seq // tm
    n_tiles = bsz * tiles_per_seq
    bf16 = jnp.bfloat16
    f32 = jnp.float32
    row = lambda g: g.reshape(1, D_MODEL).astype(f32)
    smem = pl.BlockSpec(memory_space=pltpu.SMEM)

    in_specs = [
        smem,
        smem,
        _resident((BLOCK, 2 * BLOCK)),
        pl.BlockSpec((tm, D_MODEL), lambda s: (jnp.minimum(s, n_tiles - 1), 0)),
        _resident((1, D_MODEL)),
        _resident((D_MODEL, 4 * D_MODEL)),
        _resident((CONV_K, D_MODEL)),
        _resident((D_MODEL, D_MODEL)),
        _resident((1, D_MODEL)),
        _resident((1, D_MODEL)),
        _resident((D_MODEL, 2 * KV_WIDTH)),
        _resident((1, D_MODEL)),
        _resident((D_MODEL, 2 * D_MODEL)),
        _resident((D_MODEL, D_MODEL)),
        _resident((1, D_MODEL)),
    ]
    scratch_shapes = [
        pltpu.VMEM((2, N_Q_HEADS, BLOCK, 2 * BLOCK), f32),
        pltpu.VMEM((tm, D_MODEL), bf16),
        pltpu.VMEM((tm + SUBLANES, D_MODEL), f32),
        pltpu.VMEM((tm, D_MODEL), bf16),
        pltpu.VMEM((tm, D_MODEL), f32),
        pltpu.VMEM((tm, D_MODEL), bf16),
        pltpu.VMEM((tm, D_MODEL), f32),
        pltpu.VMEM((2 * N_KV_HEADS, tm + BLOCK, LANES), bf16),
        pltpu.VMEM((2 * N_KV_HEADS, tm + BLOCK, LANES), bf16),
        pltpu.VMEM((tm, D_MODEL), bf16),
    ]
    out = pl.pallas_call(
        functools.partial(_trunk_kernel, tiles_per_seq=tiles_per_seq),
        grid=(n_tiles + 1,),
        in_specs=in_specs,
        out_specs=pl.BlockSpec((tm, D_MODEL), lambda s: (jnp.maximum(s - 1, 0), 0)),
        out_shape=jax.ShapeDtypeStruct((bsz * seq, D_MODEL), f32),
        scratch_shapes=scratch_shapes,
        compiler_params=pltpu.CompilerParams(
            dimension_semantics=("arbitrary",),
            vmem_limit_bytes=VMEM_LIMIT_BYTES),
        name="yoco_trunk",
    )(
        rel_bias.astype(f32), b_sinks.reshape(1, N_Q_HEADS).astype(f32),
        jnp.asarray(_bucket_table()), x.reshape(bsz * seq, D_MODEL),
        row(a_pre_norm), a_w_in[0].astype(bf16), a_conv_w[0].astype(f32),
        a_w_out[0].astype(bf16), row(a_post_norm),
        row(kv_norm), w_kv.astype(bf16), row(b_pre_norm), b_w_in[0].astype(bf16),
        b_w_out[0].astype(bf16), row(b_post_norm),
    )
    return out.reshape(bsz, seq, D_MODEL)
```

```python
import functools
import math

import jax
import jax.numpy as jnp
import numpy as np
from jax.experimental import pallas as pl
from jax.experimental.pallas import tpu as pltpu

D_MODEL = 1024
CONV_K = 3
HEAD_DIM = 64
N_Q_HEADS = D_MODEL // HEAD_DIM
N_KV_HEADS = 2
GROUP = N_Q_HEADS // N_KV_HEADS
KV_WIDTH = N_KV_HEADS * HEAD_DIM
BLOCK = 128
N_BUCKETS = 32
MAX_DISTANCE = 128
EPS = 1e-6
NEG_INF = -1e30

LANES = 128
SUBLANES = 8
HEADS_PER_VREG = LANES // HEAD_DIM
N_PAIRS = N_Q_HEADS // HEADS_PER_VREG
PAIRS_PER_KV = GROUP // HEADS_PER_VREG
ROW_TILE = 256
COL_CHUNK = 256
SCORE_SLOTS = 3
VMEM_LIMIT_BYTES = 52 * 1024 * 1024


def _bucket_table():
    q_loc = np.arange(BLOCK, dtype=np.int32)[:, None]
    s_loc = np.arange(2 * BLOCK, dtype=np.int32)[None, :]
    dist = q_loc + BLOCK - s_loc
    in_window = (dist >= 0) & (dist < BLOCK)
    d = np.maximum(dist, 0)
    max_exact = N_BUCKETS // 2
    df = np.maximum(d, 1).astype(np.float32)
    large = max_exact + (
        np.log(df / np.float32(max_exact)) / np.float32(math.log(MAX_DISTANCE / max_exact))
        * np.float32(N_BUCKETS - max_exact)).astype(np.int32)
    large = np.minimum(large, N_BUCKETS - 1)
    bucket = np.where(d < max_exact, d, large)
    return np.where(in_window, bucket, -1).astype(np.int32)


def _rms_scale(x):
    return x * jax.lax.rsqrt(jnp.mean(x * x, axis=-1, keepdims=True) + EPS)


def _trunk_kernel(rel_bias_ref, sinks_ref, bucket_ref, x_ref,
                  a_pre_ref, w_in_ref, conv_w_ref, w_out_a_ref, a_post_ref,
                  kv_norm_ref, w_kv_ref, b_pre_ref, w_in_b_ref, w_out_b_ref, b_post_ref,
                  out_ref,
                  bias_scr, hn_scr, v_scr, y_scr, h1_scr, q_scr, z_scr, kz_scr, vz_scr, g_scr,
                  s_scr, gate_scr,
                  *, tiles_per_seq):
    step = pl.program_id(0)
    tm = x_ref.shape[0]
    f32 = jnp.float32
    bf16 = jnp.bfloat16

    @pl.when(step == 0)
    def _init():
        bucket = bucket_ref[...]
        has_prev = jax.lax.broadcasted_iota(jnp.int32, bucket.shape, 1) >= BLOCK
        for h in range(N_Q_HEADS):
            acc = jnp.full(bucket.shape, NEG_INF, f32)
            for b in range(N_BUCKETS):
                acc = jnp.where(bucket == b, rel_bias_ref[b, h], acc)
            bias_scr[0, h] = acc
            bias_scr[1, h] = jnp.where(has_prev, acc, NEG_INF)
        v_scr[...] = jnp.zeros(v_scr.shape, f32)
        h1_scr[...] = jnp.zeros(h1_scr.shape, f32)
        q_scr[...] = jnp.zeros(q_scr.shape, bf16)
        z_scr[...] = jnp.zeros(z_scr.shape, f32)
        kz_scr[...] = jnp.zeros(kz_scr.shape, bf16)
        vz_scr[...] = jnp.zeros(vz_scr.shape, bf16)

    n_blocks = (tm // BLOCK) * N_PAIRS
    width = D_MODEL
    n_gate_dots = 4 * (width // COL_CHUNK)
    assert n_gate_dots == n_blocks

    b_seq_start = jax.lax.rem(step + (tiles_per_seq - 1), tiles_per_seq) == 0

    def block_slices(k):
        j, pair = divmod(k, N_PAIRS)
        return (slice(j * BLOCK, (j + 1) * BLOCK), slice(j * BLOCK, (j + 2) * BLOCK),
                slice(pair * LANES, (pair + 1) * LANES), j, pair)

    def b_scores(k):
        rows, key_rows, lanes, _, pair = block_slices(k)
        q_pair = q_scr[rows, lanes]
        for parity in range(HEADS_PER_VREG):
            s_scr[k % SCORE_SLOTS, parity] = jax.lax.dot_general(
                q_pair, kz_scr[(pair // PAIRS_PER_KV) * HEADS_PER_VREG + parity, key_rows, :],
                (((1,), (1,)), ((), ())), preferred_element_type=f32)

    def b_finish(k):
        rows, key_rows, lanes, j, pair = block_slices(k)
        first_block = jnp.where(b_seq_start, 1, 0) if j == 0 else 0
        o_pair = None
        for parity in range(HEADS_PER_VREG):
            head = pair * HEADS_PER_VREG + parity
            slot = (pair // PAIRS_PER_KV) * HEADS_PER_VREG + parity
            s = s_scr[k % SCORE_SLOTS, parity] + bias_scr[first_block, head]
            sink = sinks_ref[0, head]
            m = jnp.maximum(jnp.max(s, axis=-1, keepdims=True), sink)
            p = jnp.exp(s - m)
            denom = jnp.sum(p, axis=-1, keepdims=True) + jnp.exp(sink - m)
            pv = jnp.dot(p.astype(bf16), vz_scr[slot, key_rows, :], preferred_element_type=f32)
            pv = pv * (1.0 / denom)
            o_pair = pv if o_pair is None else o_pair + pv
        zg = z_scr[rows, lanes]
        g_scr[rows, lanes] = (o_pair * (zg * jax.nn.sigmoid(zg))).astype(bf16)

    a_seq_start = jax.lax.rem(step, tiles_per_seq) == 0

    def a_gate_dot(i):
        c0 = (i // 4) * COL_CHUNK
        g = i % 4
        gate_scr[g] = jnp.dot(hn_scr[...], w_in_ref[:, g * width + c0:g * width + c0 + COL_CHUNK],
                              preferred_element_type=f32)
        if g == 3:
            cols = slice(c0, c0 + COL_CHUNK)
            b_gate, c_gate, u, z = (gate_scr[i] for i in range(4))
            v_scr[SUBLANES:SUBLANES + tm, cols] = c_gate * u
            conv = (conv_w_ref[0:1, cols] * v_scr[SUBLANES - 2:SUBLANES - 2 + tm, cols]
                    + conv_w_ref[1:2, cols] * v_scr[SUBLANES - 1:SUBLANES - 1 + tm, cols]
                    + conv_w_ref[2:3, cols] * v_scr[SUBLANES:SUBLANES + tm, cols])
            y_scr[:, cols] = (b_gate * conv * (z * jax.nn.sigmoid(z))).astype(bf16)

    out_ref[...] = h1_scr[...]
    v_scr[0:SUBLANES, :] = jnp.where(a_seq_start, 0.0, v_scr[tm:tm + SUBLANES, :])
    hn_scr[...] = (_rms_scale(x_ref[...]) * a_pre_ref[...]).astype(bf16)
    b_scores(0)
    b_scores(1)
    for k in range(n_blocks):
        a_gate_dot(k)
        b_finish(k)
        if k + 2 < n_blocks:
            b_scores(k + 2)

    y = jnp.dot(y_scr[...], w_out_a_ref[...], preferred_element_type=f32)
    y2 = jnp.dot(g_scr[...], w_out_b_ref[...], preferred_element_type=f32)
    h1 = x_ref[...] + _rms_scale(y) * a_post_ref[...]
    h1_scr[...] = h1
    kz_scr[:, 0:BLOCK, :] = jnp.where(
        a_seq_start, 0.0, kz_scr[:, tm:tm + BLOCK, :].astype(f32)).astype(bf16)
    vz_scr[:, 0:BLOCK, :] = jnp.where(
        a_seq_start, 0.0, vz_scr[:, tm:tm + BLOCK, :].astype(f32)).astype(bf16)
    h1n = _rms_scale(h1)
    kv = jnp.dot((h1n * kv_norm_ref[...]).astype(bf16), w_kv_ref[...],
                 preferred_element_type=f32)
    low_half = jax.lax.broadcasted_iota(jnp.int32, (tm, LANES), 1) < HEAD_DIM
    for src, dst in ((kv[:, 0:KV_WIDTH], kz_scr), (kv[:, KV_WIDTH:2 * KV_WIDTH], vz_scr)):
        swapped = pltpu.roll(src, HEAD_DIM, axis=1)
        dst[0, BLOCK:BLOCK + tm, :] = jnp.where(low_half, src, 0.0).astype(bf16)
        dst[1, BLOCK:BLOCK + tm, :] = jnp.where(low_half, 0.0, swapped).astype(bf16)
        dst[2, BLOCK:BLOCK + tm, :] = jnp.where(low_half, swapped, 0.0).astype(bf16)
        dst[3, BLOCK:BLOCK + tm, :] = jnp.where(low_half, 0.0, src).astype(bf16)
    qn = (h1n * b_pre_ref[...]).astype(bf16)
    q = jnp.dot(qn, w_in_b_ref[:, 0:D_MODEL], preferred_element_type=f32)
    q_scr[...] = (q * (HEAD_DIM ** -0.5)).astype(bf16)
    out_ref[...] = out_ref[...] + _rms_scale(y2) * b_post_ref[...]
    z_scr[...] = jnp.dot(qn, w_in_b_ref[:, D_MODEL:2 * D_MODEL], preferred_element_type=f32)


def _resident(shape):
    return pl.BlockSpec(shape, lambda s: (0,) * len(shape), pipeline_mode=pl.Buffered(1))


@jax.jit
def kernel(x, a_pre_norm, a_w_in, a_conv_w, a_w_out, a_post_norm, kv_norm, w_kv, rel_bias,
           b_pre_norm, b_w_in, b_sinks, b_w_out, b_post_norm):
    bsz, seq, d_model = x.shape
    assert d_model == D_MODEL and seq % ROW_TILE == 0
    assert a_w_in.shape[0] == 1 and b_w_in.shape[0] == 1
    tm = ROW_TILE
    tiles_per_seq = seq // tm
    n_tiles = bsz * tiles_per_seq
    bf16 = jnp.bfloat16
    f32 = jnp.float32
    row = lambda g: g.reshape(1, D_MODEL).astype(f32)
    smem = pl.BlockSpec(memory_space=pltpu.SMEM)

    in_specs = [
        smem,
        smem,
        _resident((BLOCK, 2 * BLOCK)),
        pl.BlockSpec((tm, D_MODEL), lambda s: (jnp.minimum(s, n_tiles - 1), 0)),
        _resident((1, D_MODEL)),
        _resident((D_MODEL, 4 * D_MODEL)),
        _resident((CONV_K, D_MODEL)),
        _resident((D_MODEL, D_MODEL)),
        _resident((1, D_MODEL)),
        _resident((1, D_MODEL)),
        _resident((D_MODEL, 2 * KV_WIDTH)),
        _resident((1, D_MODEL)),
        _resident((D_MODEL, 2 * D_MODEL)),
        _resident((D_MODEL, D_MODEL)),
        _resident((1, D_MODEL)),
    ]
    scratch_shapes = [
        pltpu.VMEM((2, N_Q_HEADS, BLOCK, 2 * BLOCK), f32),
        pltpu.VMEM((tm, D_MODEL), bf16),
        pltpu.VMEM((tm + SUBLANES, D_MODEL), f32),
        pltpu.VMEM((tm, D_MODEL), bf16),
        pltpu.VMEM((tm, D_MODEL), f32),
        pltpu.VMEM((tm, D_MODEL), bf16),
        pltpu.VMEM((tm, D_MODEL), f32),
        pltpu.VMEM((2 * N_KV_HEADS, tm + BLOCK, LANES), bf16),
        pltpu.VMEM((2 * N_KV_HEADS, tm + BLOCK, LANES), bf16),
        pltpu.VMEM((tm, D_MODEL), bf16),
        pltpu.VMEM((SCORE_SLOTS, HEADS_PER_VREG, BLOCK, 2 * BLOCK), f32),
        pltpu.VMEM((4, tm, COL_CHUNK), f32),
    ]
    out = pl.pallas_call(
        functools.partial(_trunk_kernel, tiles_per_seq=tiles_per_seq),
        grid=(n_tiles + 1,),
        in_specs=in_specs,
        out_specs=pl.BlockSpec((tm, D_MODEL), lambda s: (jnp.maximum(s - 1, 0), 0)),
        out_shape=jax.ShapeDtypeStruct((bsz * seq, D_MODEL), f32),
        scratch_shapes=scratch_shapes,
        compiler_params=pltpu.CompilerParams(
            dimension_semantics=("arbitrary",),
            vmem_limit_bytes=VMEM_LIMIT_BYTES),
        name="yoco_trunk",
    )(
        rel_bias.astype(f32), b_sinks.reshape(1, N_Q_HEADS).astype(f32),
        jnp.asarray(_bucket_table()), x.reshape(bsz * seq, D_MODEL),
        row(a_pre_norm), a_w_in[0].astype(bf16), a_conv_w[0].astype(f32),
        a_w_out[0].astype(bf16), row(a_post_norm),
        row(kv_norm), w_kv.astype(bf16), row(b_pre_norm), b_w_in[0].astype(bf16),
        b_w_out[0].astype(bf16), row(b_post_norm),
    )
    return out.reshape(bsz, seq, D_MODEL)
```

```python
import functools
import math

import jax
import jax.numpy as jnp
import numpy as np
from jax.experimental import pallas as pl
from jax.experimental.pallas import tpu as pltpu

D_MODEL = 1024
CONV_K = 3
HEAD_DIM = 64
N_Q_HEADS = D_MODEL // HEAD_DIM
N_KV_HEADS = 2
GROUP = N_Q_HEADS // N_KV_HEADS
KV_WIDTH = N_KV_HEADS * HEAD_DIM
BLOCK = 128
N_BUCKETS = 32
MAX_DISTANCE = 128
EPS = 1e-6
NEG_INF = -1e30
LOG2_E = math.log2(math.e)

LANES = 128
SUBLANES = 8
HEADS_PER_VREG = LANES // HEAD_DIM
N_PAIRS = N_Q_HEADS // HEADS_PER_VREG
PAIRS_PER_KV = GROUP // HEADS_PER_VREG
ROW_TILE = 256
COL_CHUNK = 256
SCORE_SLOTS = 3
NEXT_NORM_CHUNKS = 8
PROB_SLOTS = 2
SOFTMAX_ROWS = 32
VMEM_LIMIT_BYTES = 52 * 1024 * 1024


def _bucket_table():
    q_loc = np.arange(BLOCK, dtype=np.int32)[:, None]
    s_loc = np.arange(2 * BLOCK, dtype=np.int32)[None, :]
    dist = q_loc + BLOCK - s_loc
    in_window = (dist >= 0) & (dist < BLOCK)
    d = np.maximum(dist, 0)
    max_exact = N_BUCKETS // 2
    df = np.maximum(d, 1).astype(np.float32)
    large = max_exact + (
        np.log(df / np.float32(max_exact)) / np.float32(math.log(MAX_DISTANCE / max_exact))
        * np.float32(N_BUCKETS - max_exact)).astype(np.int32)
    large = np.minimum(large, N_BUCKETS - 1)
    bucket = np.where(d < max_exact, d, large)
    return np.where(in_window, bucket, -1).astype(np.int32)


def _rms_scale(x):
    return x * jax.lax.rsqrt(jnp.mean(x * x, axis=-1, keepdims=True) + EPS)


def _trunk_kernel(rel_bias_ref, sinks_ref, bucket_ref, x_ref, x_next_ref,
                  a_pre_ref, w_in_ref, conv_w_ref, w_out_a_ref, a_post_ref,
                  kv_norm_ref, w_kv_ref, b_pre_ref, w_in_b_ref, w_out_b_ref, b_post_ref,
                  out_ref,
                  bias_scr, hn_scr, v_scr, y_scr, h1_scr, q_scr, z_scr, kz_scr, vz_scr, g_scr,
                  s_scr, gate_scr, yf_scr, p_scr, den_scr, hn_next_scr,
                  *, tiles_per_seq):
    step = pl.program_id(0)
    tm = x_ref.shape[0]
    f32 = jnp.float32
    bf16 = jnp.bfloat16

    @pl.when(step == 0)
    def _init():
        bucket = bucket_ref[...]
        for h in range(N_Q_HEADS):
            acc = jnp.full(bucket.shape, NEG_INF, f32)
            for b in range(N_BUCKETS):
                acc = jnp.where(bucket == b, rel_bias_ref[b, h] * LOG2_E, acc)
            bias_scr[h] = acc
        v_scr[...] = jnp.zeros(v_scr.shape, f32)
        h1_scr[...] = jnp.zeros(h1_scr.shape, f32)
        q_scr[...] = jnp.zeros(q_scr.shape, bf16)
        z_scr[...] = jnp.zeros(z_scr.shape, f32)
        kz_scr[...] = jnp.zeros(kz_scr.shape, bf16)
        vz_scr[...] = jnp.zeros(vz_scr.shape, bf16)
        hn_scr[...] = (_rms_scale(x_ref[...]) * a_pre_ref[...]).astype(bf16)

    n_blocks = (tm // BLOCK) * N_PAIRS
    width = D_MODEL
    n_gate_dots = 4 * (width // COL_CHUNK)
    assert n_gate_dots == n_blocks

    b_seq_start = jax.lax.rem(step + (tiles_per_seq - 1), tiles_per_seq) == 0

    def block_slices(k):
        j, pair = divmod(k, N_PAIRS)
        return (slice(j * BLOCK, (j + 1) * BLOCK), slice(j * BLOCK, (j + 2) * BLOCK),
                slice(pair * LANES, (pair + 1) * LANES), j, pair)

    def b_scores(k):
        rows, key_rows, lanes, _, pair = block_slices(k)
        q_pair = q_scr[rows, lanes]
        for parity in range(HEADS_PER_VREG):
            s_scr[k % SCORE_SLOTS, parity] = jax.lax.dot_general(
                q_pair, kz_scr[(pair // PAIRS_PER_KV) * HEADS_PER_VREG + parity, key_rows, :],
                (((1,), (1,)), ((), ())), preferred_element_type=f32)

    def b_softmax(k):
        _, _, _, j, pair = block_slices(k)
        for parity in range(HEADS_PER_VREG):
            head = pair * HEADS_PER_VREG + parity
            sink = sinks_ref[0, head] * LOG2_E
            for r0 in range(0, BLOCK, SOFTMAX_ROWS):
                sub = slice(r0, r0 + SOFTMAX_ROWS)
                s = s_scr[k % SCORE_SLOTS, parity, sub, :] + bias_scr[head, sub, :]
                if j == 0:
                    s = jnp.concatenate(
                        [jnp.where(b_seq_start, NEG_INF, s[:, 0:BLOCK]), s[:, BLOCK:2 * BLOCK]],
                        axis=1)
                m = jnp.maximum(jnp.max(s, axis=-1, keepdims=True), sink)
                p = jnp.exp2(s - m)
                denom = jnp.sum(p, axis=-1, keepdims=True) + jnp.exp2(sink - m)
                p_scr[k % PROB_SLOTS, parity, sub, :] = p.astype(bf16)
                den_scr[k % PROB_SLOTS, parity, sub, :] = jnp.broadcast_to(
                    denom, (SOFTMAX_ROWS, LANES))

    def b_pv(k):
        rows, key_rows, lanes, _, pair = block_slices(k)
        o_pair = None
        for parity in range(HEADS_PER_VREG):
            slot = (pair // PAIRS_PER_KV) * HEADS_PER_VREG + parity
            pv = jnp.dot(p_scr[k % PROB_SLOTS, parity], vz_scr[slot, key_rows, :],
                         preferred_element_type=f32)
            pv = pv * (1.0 / den_scr[k % PROB_SLOTS, parity])
            o_pair = pv if o_pair is None else o_pair + pv
        zg = z_scr[rows, lanes]
        g_scr[rows, lanes] = (o_pair * (zg * jax.nn.sigmoid(zg))).astype(bf16)

    a_seq_start = jax.lax.rem(step, tiles_per_seq) == 0

    def a_gate_dot(i):
        c0 = (i // 4) * COL_CHUNK
        cols = slice(c0, c0 + COL_CHUNK)
        phase = i % 4
        g = (1, 2, 0, 3)[phase]
        r = jnp.dot(hn_scr[...], w_in_ref[:, g * width + c0:g * width + c0 + COL_CHUNK],
                    preferred_element_type=f32)
        if phase == 0:
            gate_scr[...] = r
        elif phase == 1:
            v_scr[SUBLANES:SUBLANES + tm, cols] = gate_scr[...] * r
            gate_scr[...] = (conv_w_ref[0:1, cols] * v_scr[SUBLANES - 2:SUBLANES - 2 + tm, cols]
                             + conv_w_ref[1:2, cols] * v_scr[SUBLANES - 1:SUBLANES - 1 + tm, cols]
                             + conv_w_ref[2:3, cols] * v_scr[SUBLANES:SUBLANES + tm, cols])
        elif phase == 2:
            gate_scr[...] = r * gate_scr[...]
        else:
            y_scr[:, cols] = (gate_scr[...] * (r * jax.nn.sigmoid(r))).astype(bf16)

    out_ref[...] = h1_scr[...]
    v_scr[0:SUBLANES, :] = jnp.where(a_seq_start, 0.0, v_scr[tm:tm + SUBLANES, :])
    def a_out_proj(c0, c1):
        yf_scr[:, c0:c1] = jnp.dot(y_scr[...], w_out_a_ref[:, c0:c1], preferred_element_type=f32)

    b_scores(0)
    b_scores(1)
    a_gate_dot(0)
    for k in range(n_blocks):
        b_softmax(k)
        if k >= n_blocks - NEXT_NORM_CHUNKS:
            nrows = tm // NEXT_NORM_CHUNKS
            sub = slice((k - (n_blocks - NEXT_NORM_CHUNKS)) * nrows,
                        (k - (n_blocks - NEXT_NORM_CHUNKS) + 1) * nrows)
            hn_next_scr[sub, :] = (_rms_scale(x_next_ref[sub, :]) * a_pre_ref[...]).astype(bf16)
        if k + 1 < n_gate_dots:
            a_gate_dot(k + 1)
        else:
            a_out_proj(0, D_MODEL // 2)
        if k >= 1:
            b_pv(k - 1)
        if k + 2 < n_blocks:
            b_scores(k + 2)
    a_out_proj(D_MODEL // 2, 3 * D_MODEL // 4)
    b_pv(n_blocks - 1)
    a_out_proj(3 * D_MODEL // 4, D_MODEL)

    y2 = jnp.dot(g_scr[...], w_out_b_ref[...], preferred_element_type=f32)
    h1 = x_ref[...] + _rms_scale(yf_scr[...]) * a_post_ref[...]
    h1_scr[...] = h1
    kz_scr[:, 0:BLOCK, :] = jnp.where(
        a_seq_start, 0.0, kz_scr[:, tm:tm + BLOCK, :].astype(f32)).astype(bf16)
    vz_scr[:, 0:BLOCK, :] = jnp.where(
        a_seq_start, 0.0, vz_scr[:, tm:tm + BLOCK, :].astype(f32)).astype(bf16)
    h1_inv_rms = jax.lax.rsqrt(jnp.mean(h1 * h1, axis=-1, keepdims=True) + EPS)
    kv = jnp.dot((h1 * kv_norm_ref[...]).astype(bf16), w_kv_ref[...],
                 preferred_element_type=f32) * h1_inv_rms
    low_half = jax.lax.broadcasted_iota(jnp.int32, (tm, LANES), 1) < HEAD_DIM
    for src, dst in ((kv[:, 0:KV_WIDTH], kz_scr), (kv[:, KV_WIDTH:2 * KV_WIDTH], vz_scr)):
        swapped = pltpu.roll(src, HEAD_DIM, axis=1)
        dst[0, BLOCK:BLOCK + tm, :] = jnp.where(low_half, src, 0.0).astype(bf16)
        dst[1, BLOCK:BLOCK + tm, :] = jnp.where(low_half, 0.0, swapped).astype(bf16)
        dst[2, BLOCK:BLOCK + tm, :] = jnp.where(low_half, swapped, 0.0).astype(bf16)
        dst[3, BLOCK:BLOCK + tm, :] = jnp.where(low_half, 0.0, src).astype(bf16)
    qn = (h1 * b_pre_ref[...]).astype(bf16)
    q = jnp.dot(qn, w_in_b_ref[:, 0:D_MODEL], preferred_element_type=f32)
    q_scr[...] = (q * (h1_inv_rms * (HEAD_DIM ** -0.5 * LOG2_E))).astype(bf16)
    out_ref[...] = out_ref[...] + _rms_scale(y2) * b_post_ref[...]
    z_scr[...] = jnp.dot(qn, w_in_b_ref[:, D_MODEL:2 * D_MODEL],
                         preferred_element_type=f32) * h1_inv_rms
    hn_scr[...] = hn_next_scr[...]


def _resident(shape):
    return pl.BlockSpec(shape, lambda s: (0,) * len(shape), pipeline_mode=pl.Buffered(1))


@jax.jit
def kernel(x, a_pre_norm, a_w_in, a_conv_w, a_w_out, a_post_norm, kv_norm, w_kv, rel_bias,
           b_pre_norm, b_w_in, b_sinks, b_w_out, b_post_norm):
    bsz, seq, d_model = x.shape
    assert d_model == D_MODEL and seq % ROW_TILE == 0
    assert a_w_in.shape[0] == 1 and b_w_in.shape[0] == 1
    tm = ROW_TILE
    tiles_per_seq = seq // tm
    n_tiles = bsz * tiles_per_seq
    bf16 = jnp.bfloat16
    f32 = jnp.float32
    row = lambda g: g.reshape(1, D_MODEL).astype(f32)
    smem = pl.BlockSpec(memory_space=pltpu.SMEM)

    in_specs = [
        smem,
        smem,
        _resident((BLOCK, 2 * BLOCK)),
        pl.BlockSpec((tm, D_MODEL), lambda s: (jnp.minimum(s, n_tiles - 1), 0)),
        pl.BlockSpec((tm, D_MODEL), lambda s: (jnp.minimum(s + 1, n_tiles - 1), 0)),
        _resident((1, D_MODEL)),
        _resident((D_MODEL, 4 * D_MODEL)),
        _resident((CONV_K, D_MODEL)),
        _resident((D_MODEL, D_MODEL)),
        _resident((1, D_MODEL)),
        _resident((1, D_MODEL)),
        _resident((D_MODEL, 2 * KV_WIDTH)),
        _resident((1, D_MODEL)),
        _resident((D_MODEL, 2 * D_MODEL)),
        _resident((D_MODEL, D_MODEL)),
        _resident((1, D_MODEL)),
    ]
    scratch_shapes = [
        pltpu.VMEM((N_Q_HEADS, BLOCK, 2 * BLOCK), f32),
        pltpu.VMEM((tm, D_MODEL), bf16),
        pltpu.VMEM((tm + SUBLANES, D_MODEL), f32),
        pltpu.VMEM((tm, D_MODEL), bf16),
        pltpu.VMEM((tm, D_MODEL), f32),
        pltpu.VMEM((tm, D_MODEL), bf16),
        pltpu.VMEM((tm, D_MODEL), f32),
        pltpu.VMEM((2 * N_KV_HEADS, tm + BLOCK, LANES), bf16),
        pltpu.VMEM((2 * N_KV_HEADS, tm + BLOCK, LANES), bf16),
        pltpu.VMEM((tm, D_MODEL), bf16),
        pltpu.VMEM((SCORE_SLOTS, HEADS_PER_VREG, BLOCK, 2 * BLOCK), f32),
        pltpu.VMEM((tm, COL_CHUNK), f32),
        pltpu.VMEM((tm, D_MODEL), f32),
        pltpu.VMEM((PROB_SLOTS, HEADS_PER_VREG, BLOCK, 2 * BLOCK), bf16),
        pltpu.VMEM((PROB_SLOTS, HEADS_PER_VREG, BLOCK, LANES), f32),
        pltpu.VMEM((tm, D_MODEL), bf16),
    ]
    out = pl.pallas_call(
        functools.partial(_trunk_kernel, tiles_per_seq=tiles_per_seq),
        grid=(n_tiles + 1,),
        in_specs=in_specs,
        out_specs=pl.BlockSpec((tm, D_MODEL), lambda s: (jnp.maximum(s - 1, 0), 0)),
        out_shape=jax.ShapeDtypeStruct((bsz * seq, D_MODEL), f32),
        scratch_shapes=scratch_shapes,
        compiler_params=pltpu.CompilerParams(
            dimension_semantics=("arbitrary",),
            vmem_limit_bytes=VMEM_LIMIT_BYTES),
        name="yoco_trunk",
    )(
        rel_bias.astype(f32), b_sinks.reshape(1, N_Q_HEADS).astype(f32),
        jnp.asarray(_bucket_table()), x.reshape(bsz * seq, D_MODEL), x.reshape(bsz * seq, D_MODEL),
        row(a_pre_norm), a_w_in[0].astype(bf16), a_conv_w[0].astype(f32),
        a_w_out[0].astype(bf16), row(a_post_norm),
        row(kv_norm), w_kv.astype(bf16), row(b_pre_norm), b_w_in[0].astype(bf16),
        b_w_out[0].astype(bf16), row(b_post_norm),
    )
    return out.reshape(bsz, seq, D_MODEL)
```

```python
import functools
import math

import jax
import jax.numpy as jnp
import numpy as np
from jax.experimental import pallas as pl
from jax.experimental.pallas import tpu as pltpu

D_MODEL = 1024
CONV_K = 3
HEAD_DIM = 64
N_Q_HEADS = D_MODEL // HEAD_DIM
N_KV_HEADS = 2
GROUP = N_Q_HEADS // N_KV_HEADS
KV_WIDTH = N_KV_HEADS * HEAD_DIM
BLOCK = 128
N_BUCKETS = 32
MAX_DISTANCE = 128
EPS = 1e-6
NEG_INF = -1e30

LANES = 128
SUBLANES = 8
HEADS_PER_VREG = LANES // HEAD_DIM
N_PAIRS = N_Q_HEADS // HEADS_PER_VREG
PAIRS_PER_KV = GROUP // HEADS_PER_VREG
ROW_TILE = 512
COL_CHUNK = 256
SCORE_SLOTS = 3
VMEM_LIMIT_BYTES = 56 * 1024 * 1024


def _bucket_table():
    q_loc = np.arange(BLOCK, dtype=np.int32)[:, None]
    s_loc = np.arange(2 * BLOCK, dtype=np.int32)[None, :]
    dist = q_loc + BLOCK - s_loc
    in_window = (dist >= 0) & (dist < BLOCK)
    d = np.maximum(dist, 0)
    max_exact = N_BUCKETS // 2
    df = np.maximum(d, 1).astype(np.float32)
    large = max_exact + (
        np.log(df / np.float32(max_exact)) / np.float32(math.log(MAX_DISTANCE / max_exact))
        * np.float32(N_BUCKETS - max_exact)).astype(np.int32)
    large = np.minimum(large, N_BUCKETS - 1)
    bucket = np.where(d < max_exact, d, large)
    return np.where(in_window, bucket, -1).astype(np.int32)


def _rms_scale(x):
    return x * jax.lax.rsqrt(jnp.mean(x * x, axis=-1, keepdims=True) + EPS)


def _trunk_kernel(rel_bias_ref, sinks_ref, bucket_ref, x_ref,
                  a_pre_ref, w_in_ref, conv_w_ref, w_out_a_ref, a_post_ref,
                  kv_norm_ref, w_kv_ref, b_pre_ref, w_in_b_ref, w_out_b_ref, b_post_ref,
                  out_ref,
                  bias_scr, hn_scr, v_scr, y_scr, h1_scr, q_scr, z_scr, kz_scr, vz_scr, g_scr,
                  s_scr, gate_scr,
                  *, tiles_per_seq):
    step = pl.program_id(0)
    tm = x_ref.shape[0]
    f32 = jnp.float32
    bf16 = jnp.bfloat16

    @pl.when(step == 0)
    def _init():
        bucket = bucket_ref[...]
        has_prev = jax.lax.broadcasted_iota(jnp.int32, bucket.shape, 1) >= BLOCK
        for h in range(N_Q_HEADS):
            acc = jnp.full(bucket.shape, NEG_INF, f32)
            for b in range(N_BUCKETS):
                acc = jnp.where(bucket == b, rel_bias_ref[b, h], acc)
            bias_scr[0, h] = acc
            bias_scr[1, h] = jnp.where(has_prev, acc, NEG_INF)
        v_scr[...] = jnp.zeros(v_scr.shape, f32)
        h1_scr[...] = jnp.zeros(h1_scr.shape, f32)
        q_scr[...] = jnp.zeros(q_scr.shape, bf16)
        z_scr[...] = jnp.zeros(z_scr.shape, f32)
        kz_scr[...] = jnp.zeros(kz_scr.shape, bf16)
        vz_scr[...] = jnp.zeros(vz_scr.shape, bf16)

    n_blocks = (tm // BLOCK) * N_PAIRS
    width = D_MODEL
    n_gate_dots = 4 * (width // COL_CHUNK)
    blocks_per_round, rem = divmod(n_blocks, n_gate_dots)
    assert rem == 0 and blocks_per_round >= 1

    b_seq_start = jax.lax.rem(step + (tiles_per_seq - 1), tiles_per_seq) == 0

    def block_slices(k):
        j, pair = divmod(k, N_PAIRS)
        return (slice(j * BLOCK, (j + 1) * BLOCK), slice(j * BLOCK, (j + 2) * BLOCK),
                slice(pair * LANES, (pair + 1) * LANES), j, pair)

    def b_scores(k):
        rows, key_rows, lanes, _, pair = block_slices(k)
        q_pair = q_scr[rows, lanes]
        for parity in range(HEADS_PER_VREG):
            s_scr[k % SCORE_SLOTS, parity] = jax.lax.dot_general(
                q_pair, kz_scr[(pair // PAIRS_PER_KV) * HEADS_PER_VREG + parity, key_rows, :],
                (((1,), (1,)), ((), ())), preferred_element_type=f32)

    def b_finish(k):
        rows, key_rows, lanes, j, pair = block_slices(k)
        first_block = jnp.where(b_seq_start, 1, 0) if j == 0 else 0
        o_pair = None
        for parity in range(HEADS_PER_VREG):
            head = pair * HEADS_PER_VREG + parity
            slot = (pair // PAIRS_PER_KV) * HEADS_PER_VREG + parity
            s = s_scr[k % SCORE_SLOTS, parity] + bias_scr[first_block, head]
            sink = sinks_ref[0, head]
            m = jnp.maximum(jnp.max(s, axis=-1, keepdims=True), sink)
            p = jnp.exp(s - m)
            denom = jnp.sum(p, axis=-1, keepdims=True) + jnp.exp(sink - m)
            pv = jnp.dot(p.astype(bf16), vz_scr[slot, key_rows, :], preferred_element_type=f32)
            pv = pv * (1.0 / denom)
            o_pair = pv if o_pair is None else o_pair + pv
        zg = z_scr[rows, lanes]
        g_scr[rows, lanes] = (o_pair * (zg * jax.nn.sigmoid(zg))).astype(bf16)

    a_seq_start = jax.lax.rem(step, tiles_per_seq) == 0

    def a_gate_dot(i):
        c0 = (i // 4) * COL_CHUNK
        g = i % 4
        gate_scr[g] = jnp.dot(hn_scr[...], w_in_ref[:, g * width + c0:g * width + c0 + COL_CHUNK],
                              preferred_element_type=f32)
        if g == 3:
            cols = slice(c0, c0 + COL_CHUNK)
            b_gate, c_gate, u, z = (gate_scr[i] for i in range(4))
            v_scr[SUBLANES:SUBLANES + tm, cols] = c_gate * u
            conv = (conv_w_ref[0:1, cols] * v_scr[SUBLANES - 2:SUBLANES - 2 + tm, cols]
                    + conv_w_ref[1:2, cols] * v_scr[SUBLANES - 1:SUBLANES - 1 + tm, cols]
                    + conv_w_ref[2:3, cols] * v_scr[SUBLANES:SUBLANES + tm, cols])
            y_scr[:, cols] = (b_gate * conv * (z * jax.nn.sigmoid(z))).astype(bf16)

    out_ref[...] = h1_scr[...]
    v_scr[0:SUBLANES, :] = jnp.where(a_seq_start, 0.0, v_scr[tm:tm + SUBLANES, :])
    hn_scr[...] = (_rms_scale(x_ref[...]) * a_pre_ref[...]).astype(bf16)
    for k in range(SCORE_SLOTS - 1):
        b_scores(k)
    for r in range(n_gate_dots):
        a_gate_dot(r)
        for k in range(r * blocks_per_round, (r + 1) * blocks_per_round):
            b_finish(k)
            if k + SCORE_SLOTS - 1 < n_blocks:
                b_scores(k + SCORE_SLOTS - 1)

    y = jnp.dot(y_scr[...], w_out_a_ref[...], preferred_element_type=f32)
    y2 = jnp.dot(g_scr[...], w_out_b_ref[...], preferred_element_type=f32)
    h1 = x_ref[...] + _rms_scale(y) * a_post_ref[...]
    h1_scr[...] = h1
    kz_scr[:, 0:BLOCK, :] = jnp.where(
        a_seq_start, 0.0, kz_scr[:, tm:tm + BLOCK, :].astype(f32)).astype(bf16)
    vz_scr[:, 0:BLOCK, :] = jnp.where(
        a_seq_start, 0.0, vz_scr[:, tm:tm + BLOCK, :].astype(f32)).astype(bf16)
    h1n = _rms_scale(h1)
    kv = jnp.dot((h1n * kv_norm_ref[...]).astype(bf16), w_kv_ref[...],
                 preferred_element_type=f32)
    low_half = jax.lax.broadcasted_iota(jnp.int32, (tm, LANES), 1) < HEAD_DIM
    for src, dst in ((kv[:, 0:KV_WIDTH], kz_scr), (kv[:, KV_WIDTH:2 * KV_WIDTH], vz_scr)):
        swapped = pltpu.roll(src, HEAD_DIM, axis=1)
        dst[0, BLOCK:BLOCK + tm, :] = jnp.where(low_half, src, 0.0).astype(bf16)
        dst[1, BLOCK:BLOCK + tm, :] = jnp.where(low_half, 0.0, swapped).astype(bf16)
        dst[2, BLOCK:BLOCK + tm, :] = jnp.where(low_half, swapped, 0.0).astype(bf16)
        dst[3, BLOCK:BLOCK + tm, :] = jnp.where(low_half, 0.0, src).astype(bf16)
    qn = (h1n * b_pre_ref[...]).astype(bf16)
    q = jnp.dot(qn, w_in_b_ref[:, 0:D_MODEL], preferred_element_type=f32)
    q_scr[...] = (q * (HEAD_DIM ** -0.5)).astype(bf16)
    out_ref[...] = out_ref[...] + _rms_scale(y2) * b_post_ref[...]
    z_scr[...] = jnp.dot(qn, w_in_b_ref[:, D_MODEL:2 * D_MODEL], preferred_element_type=f32)


def _resident(shape):
    return pl.BlockSpec(shape, lambda s: (0,) * len(shape), pipeline_mode=pl.Buffered(1))


@jax.jit
def kernel(x, a_pre_norm, a_w_in, a_conv_w, a_w_out, a_post_norm, kv_norm, w_kv, rel_bias,
           b_pre_norm, b_w_in, b_sinks, b_w_out, b_post_norm):
    bsz, seq, d_model = x.shape
    assert d_model == D_MODEL and seq % ROW_TILE == 0
    assert a_w_in.shape[0] == 1 and b_w_in.shape[0] == 1
    tm = ROW_TILE
    tiles_per_seq = seq // tm
    n_tiles = bsz * tiles_per_seq
    bf16 = jnp.bfloat16
    f32 = jnp.float32
    row = lambda g: g.reshape(1, D_MODEL).astype(f32)
    smem = pl.BlockSpec(memory_space=pltpu.SMEM)

    in_specs = [
        smem,
        smem,
        _resident((BLOCK, 2 * BLOCK)),
        pl.BlockSpec((tm, D_MODEL), lambda s: (jnp.minimum(s, n_tiles - 1), 0)),
        _resident((1, D_MODEL)),
        _resident((D_MODEL, 4 * D_MODEL)),
        _resident((CONV_K, D_MODEL)),
        _resident((D_MODEL, D_MODEL)),
        _resident((1, D_MODEL)),
        _resident((1, D_MODEL)),
        _resident((D_MODEL, 2 * KV_WIDTH)),
        _resident((1, D_MODEL)),
        _resident((D_MODEL, 2 * D_MODEL)),
        _resident((D_MODEL, D_MODEL)),
        _resident((1, D_MODEL)),
    ]
    scratch_shapes = [
        pltpu.VMEM((2, N_Q_HEADS, BLOCK, 2 * BLOCK), f32),
        pltpu.VMEM((tm, D_MODEL), bf16),
        pltpu.VMEM((tm + SUBLANES, D_MODEL), f32),
        pltpu.VMEM((tm, D_MODEL), bf16),
        pltpu.VMEM((tm, D_MODEL), f32),
        pltpu.VMEM((tm, D_MODEL), bf16),
        pltpu.VMEM((tm, D_MODEL), f32),
        pltpu.VMEM((2 * N_KV_HEADS, tm + BLOCK, LANES), bf16),
        pltpu.VMEM((2 * N_KV_HEADS, tm + BLOCK, LANES), bf16),
        pltpu.VMEM((tm, D_MODEL), bf16),
        pltpu.VMEM((SCORE_SLOTS, HEADS_PER_VREG, BLOCK, 2 * BLOCK), f32),
        pltpu.VMEM((4, tm, COL_CHUNK), f32),
    ]
    out = pl.pallas_call(
        functools.partial(_trunk_kernel, tiles_per_seq=tiles_per_seq),
        grid=(n_tiles + 1,),
        in_specs=in_specs,
        out_specs=pl.BlockSpec((tm, D_MODEL), lambda s: (jnp.maximum(s - 1, 0), 0)),
        out_shape=jax.ShapeDtypeStruct((bsz * seq, D_MODEL), f32),
        scratch_shapes=scratch_shapes,
        compiler_params=pltpu.CompilerParams(
            dimension_semantics=("arbitrary",),
            vmem_limit_bytes=VMEM_LIMIT_BYTES),
        name="yoco_trunk",
    )(
        rel_bias.astype(f32), b_sinks.reshape(1, N_Q_HEADS).astype(f32),
        jnp.asarray(_bucket_table()), x.reshape(bsz * seq, D_MODEL),
        row(a_pre_norm), a_w_in[0].astype(bf16), a_conv_w[0].astype(f32),
        a_w_out[0].astype(bf16), row(a_post_norm),
        row(kv_norm), w_kv.astype(bf16), row(b_pre_norm), b_w_in[0].astype(bf16),
        b_w_out[0].astype(bf16), row(b_post_norm),
    )
    return out.reshape(bsz, seq, D_MODEL)
```

```python
import functools
import math

import jax
import jax.numpy as jnp
import numpy as np
from jax.experimental import pallas as pl
from jax.experimental.pallas import tpu as pltpu

D_MODEL = 1024
CONV_K = 3
HEAD_DIM = 64
N_Q_HEADS = D_MODEL // HEAD_DIM
N_KV_HEADS = 2
GROUP = N_Q_HEADS // N_KV_HEADS
KV_WIDTH = N_KV_HEADS * HEAD_DIM
BLOCK = 128
N_BUCKETS = 32
MAX_DISTANCE = 128
EPS = 1e-6
NEG_INF = -1e30

LANES = 128
SUBLANES = 8
HEADS_PER_VREG = LANES // HEAD_DIM
N_PAIRS = N_Q_HEADS // HEADS_PER_VREG
PAIRS_PER_KV = GROUP // HEADS_PER_VREG
ROW_TILE = 256
COL_CHUNK = 256
SCORE_SLOTS = 3
WEIGHT_CHUNK = 512
STAGE_SLOTS = 3
VMEM_LIMIT_BYTES = 56 * 1024 * 1024


def _bucket_table():
    q_loc = np.arange(BLOCK, dtype=np.int32)[:, None]
    s_loc = np.arange(2 * BLOCK, dtype=np.int32)[None, :]
    dist = q_loc + BLOCK - s_loc
    in_window = (dist >= 0) & (dist < BLOCK)
    d = np.maximum(dist, 0)
    max_exact = N_BUCKETS // 2
    df = np.maximum(d, 1).astype(np.float32)
    large = max_exact + (
        np.log(df / np.float32(max_exact)) / np.float32(math.log(MAX_DISTANCE / max_exact))
        * np.float32(N_BUCKETS - max_exact)).astype(np.int32)
    large = np.minimum(large, N_BUCKETS - 1)
    bucket = np.where(d < max_exact, d, large)
    return np.where(in_window, bucket, -1).astype(np.int32)


def _rms_scale(x):
    return x * jax.lax.rsqrt(jnp.mean(x * x, axis=-1, keepdims=True) + EPS)


def _trunk_kernel(rel_bias_ref, sinks_ref, bucket_ref, vecs_ref, x_ref,
                  w_in_hbm, w_out_a_hbm, w_kv_hbm, w_in_b_hbm, w_out_b_hbm,
                  out_ref,
                  w_in_ref, w_out_a_ref, w_kv_ref, w_in_b_ref, w_out_b_ref, stage_scr, stage_sem,
                  bias_scr, hn_scr, v_scr, y_scr, h1_scr, q_scr, z_scr, kz_scr, vz_scr, g_scr,
                  s_scr, gate_scr,
                  *, tiles_per_seq):
    step = pl.program_id(0)
    tm = x_ref.shape[0]
    f32 = jnp.float32
    bf16 = jnp.bfloat16
    a_pre_ref, a_post_ref, kv_norm_ref, b_pre_ref, b_post_ref = (
        vecs_ref.at[i:i + 1, :] for i in range(5))
    conv_w_ref = vecs_ref.at[5:5 + CONV_K, :]

    @pl.when(step == 0)
    def _init():
        bucket = bucket_ref[...]
        has_prev = jax.lax.broadcasted_iota(jnp.int32, bucket.shape, 1) >= BLOCK

        def build_bias(h):
            acc = jnp.full(bucket.shape, NEG_INF, f32)
            for b in range(N_BUCKETS):
                acc = jnp.where(bucket == b, rel_bias_ref[b, h], acc)
            bias_scr[0, h] = acc
            bias_scr[1, h] = jnp.where(has_prev, acc, NEG_INF)

        chunks = [(src, c0, min(WEIGHT_CHUNK, src.shape[1]), dst)
                  for src, dst in ((w_in_hbm, w_in_ref), (w_out_a_hbm, w_out_a_ref),
                                   (w_kv_hbm, w_kv_ref), (w_in_b_hbm, w_in_b_ref),
                                   (w_out_b_hbm, w_out_b_ref))
                  for c0 in range(0, src.shape[1], WEIGHT_CHUNK)]

        def chunk_copy(i):
            src, c0, width, _ = chunks[i]
            slot = i % STAGE_SLOTS
            return pltpu.make_async_copy(src.at[:, c0:c0 + width],
                                         stage_scr.at[slot, :, 0:width], stage_sem.at[slot])

        for i in range(min(STAGE_SLOTS, len(chunks))):
            chunk_copy(i).start()
        for i, (_, c0, width, dst) in enumerate(chunks):
            if i < N_Q_HEADS:
                build_bias(i)
            chunk_copy(i).wait()
            dst[:, c0:c0 + width] = stage_scr[i % STAGE_SLOTS, :, 0:width].astype(bf16)
            if i + STAGE_SLOTS < len(chunks):
                chunk_copy(i + STAGE_SLOTS).start()
        for h in range(len(chunks), N_Q_HEADS):
            build_bias(h)
        v_scr[...] = jnp.zeros(v_scr.shape, f32)
        h1_scr[...] = jnp.zeros(h1_scr.shape, f32)
        q_scr[...] = jnp.zeros(q_scr.shape, bf16)
        z_scr[...] = jnp.zeros(z_scr.shape, f32)
        kz_scr[...] = jnp.zeros(kz_scr.shape, bf16)
        vz_scr[...] = jnp.zeros(vz_scr.shape, bf16)

    n_blocks = (tm // BLOCK) * N_PAIRS
    width = D_MODEL
    n_gate_dots = 4 * (width // COL_CHUNK)
    blocks_per_round, rem = divmod(n_blocks, n_gate_dots)
    assert rem == 0 and blocks_per_round >= 1

    b_seq_start = jax.lax.rem(step + (tiles_per_seq - 1), tiles_per_seq) == 0

    def block_slices(k):
        j, pair = divmod(k, N_PAIRS)
        return (slice(j * BLOCK, (j + 1) * BLOCK), slice(j * BLOCK, (j + 2) * BLOCK),
                slice(pair * LANES, (pair + 1) * LANES), j, pair)

    def b_scores(k):
        rows, key_rows, lanes, _, pair = block_slices(k)
        q_pair = q_scr[rows, lanes]
        for parity in range(HEADS_PER_VREG):
            s_scr[k % SCORE_SLOTS, parity] = jax.lax.dot_general(
                q_pair, kz_scr[(pair // PAIRS_PER_KV) * HEADS_PER_VREG + parity, key_rows, :],
                (((1,), (1,)), ((), ())), preferred_element_type=f32)

    def b_finish(k):
        rows, key_rows, lanes, j, pair = block_slices(k)
        first_block = jnp.where(b_seq_start, 1, 0) if j == 0 else 0
        o_pair = None
        for parity in range(HEADS_PER_VREG):
            head = pair * HEADS_PER_VREG + parity
            slot = (pair // PAIRS_PER_KV) * HEADS_PER_VREG + parity
            s = s_scr[k % SCORE_SLOTS, parity] + bias_scr[first_block, head]
            sink = sinks_ref[0, head]
            m = jnp.maximum(jnp.max(s, axis=-1, keepdims=True), sink)
            p = jnp.exp(s - m)
            denom = jnp.sum(p, axis=-1, keepdims=True) + jnp.exp(sink - m)
            pv = jnp.dot(p.astype(bf16), vz_scr[slot, key_rows, :], preferred_element_type=f32)
            pv = pv * (1.0 / denom)
            o_pair = pv if o_pair is None else o_pair + pv
        zg = z_scr[rows, lanes]
        g_scr[rows, lanes] = (o_pair * (zg * jax.nn.sigmoid(zg))).astype(bf16)

    a_seq_start = jax.lax.rem(step, tiles_per_seq) == 0

    def a_gate_dot(i):
        c0 = (i // 4) * COL_CHUNK
        g = i % 4
        gate_scr[g] = jnp.dot(hn_scr[...], w_in_ref[:, g * width + c0:g * width + c0 + COL_CHUNK],
                              preferred_element_type=f32)
        if g == 3:
            cols = slice(c0, c0 + COL_CHUNK)
            b_gate, c_gate, u, z = (gate_scr[i] for i in range(4))
            v_scr[SUBLANES:SUBLANES + tm, cols] = c_gate * u
            conv = (conv_w_ref[0:1, cols] * v_scr[SUBLANES - 2:SUBLANES - 2 + tm, cols]
                    + conv_w_ref[1:2, cols] * v_scr[SUBLANES - 1:SUBLANES - 1 + tm, cols]
                    + conv_w_ref[2:3, cols] * v_scr[SUBLANES:SUBLANES + tm, cols])
            y_scr[:, cols] = (b_gate * conv * (z * jax.nn.sigmoid(z))).astype(bf16)

    out_ref[...] = h1_scr[...]
    v_scr[0:SUBLANES, :] = jnp.where(a_seq_start, 0.0, v_scr[tm:tm + SUBLANES, :])
    hn_scr[...] = (_rms_scale(x_ref[...]) * a_pre_ref[...]).astype(bf16)
    for k in range(SCORE_SLOTS - 1):
        b_scores(k)
    for r in range(n_gate_dots):
        a_gate_dot(r)
        for k in range(r * blocks_per_round, (r + 1) * blocks_per_round):
            b_finish(k)
            if k + SCORE_SLOTS - 1 < n_blocks:
                b_scores(k + SCORE_SLOTS - 1)

    y = jnp.dot(y_scr[...], w_out_a_ref[...], preferred_element_type=f32)
    y2 = jnp.dot(g_scr[...], w_out_b_ref[...], preferred_element_type=f32)
    h1 = x_ref[...] + _rms_scale(y) * a_post_ref[...]
    h1_scr[...] = h1
    kz_scr[:, 0:BLOCK, :] = jnp.where(
        a_seq_start, 0.0, kz_scr[:, tm:tm + BLOCK, :].astype(f32)).astype(bf16)
    vz_scr[:, 0:BLOCK, :] = jnp.where(
        a_seq_start, 0.0, vz_scr[:, tm:tm + BLOCK, :].astype(f32)).astype(bf16)
    h1n = _rms_scale(h1)
    kv = jnp.dot((h1n * kv_norm_ref[...]).astype(bf16), w_kv_ref[...],
                 preferred_element_type=f32)
    low_half = jax.lax.broadcasted_iota(jnp.int32, (tm, LANES), 1) < HEAD_DIM
    for src, dst in ((kv[:, 0:KV_WIDTH], kz_scr), (kv[:, KV_WIDTH:2 * KV_WIDTH], vz_scr)):
        swapped = pltpu.roll(src, HEAD_DIM, axis=1)
        dst[0, BLOCK:BLOCK + tm, :] = jnp.where(low_half, src, 0.0).astype(bf16)
        dst[1, BLOCK:BLOCK + tm, :] = jnp.where(low_half, 0.0, swapped).astype(bf16)
        dst[2, BLOCK:BLOCK + tm, :] = jnp.where(low_half, swapped, 0.0).astype(bf16)
        dst[3, BLOCK:BLOCK + tm, :] = jnp.where(low_half, 0.0, src).astype(bf16)
    qn = (h1n * b_pre_ref[...]).astype(bf16)
    q = jnp.dot(qn, w_in_b_ref[:, 0:D_MODEL], preferred_element_type=f32)
    q_scr[...] = (q * (HEAD_DIM ** -0.5)).astype(bf16)
    out_ref[...] = out_ref[...] + _rms_scale(y2) * b_post_ref[...]
    z_scr[...] = jnp.dot(qn, w_in_b_ref[:, D_MODEL:2 * D_MODEL], preferred_element_type=f32)


def _resident(shape):
    return pl.BlockSpec(shape, lambda s: (0,) * len(shape), pipeline_mode=pl.Buffered(1))


@jax.jit
def kernel(x, a_pre_norm, a_w_in, a_conv_w, a_w_out, a_post_norm, kv_norm, w_kv, rel_bias,
           b_pre_norm, b_w_in, b_sinks, b_w_out, b_post_norm):
    bsz, seq, d_model = x.shape
    assert d_model == D_MODEL and seq % ROW_TILE == 0
    assert a_w_in.shape[0] == 1 and b_w_in.shape[0] == 1
    tm = ROW_TILE
    tiles_per_seq = seq // tm
    n_tiles = bsz * tiles_per_seq
    bf16 = jnp.bfloat16
    f32 = jnp.float32
    smem = pl.BlockSpec(memory_space=pltpu.SMEM)
    hbm = pl.BlockSpec(memory_space=pl.ANY)
    weights = (a_w_in[0], a_w_out[0], w_kv, b_w_in[0], b_w_out[0])
    assert all(w.dtype == f32 and w.shape[0] == D_MODEL for w in weights)
    vecs = jnp.concatenate(
        [g.reshape(1, D_MODEL) for g in (a_pre_norm, a_post_norm, kv_norm, b_pre_norm, b_post_norm)]
        + [a_conv_w[0]], axis=0).astype(f32)

    in_specs = [
        smem,
        smem,
        _resident((BLOCK, 2 * BLOCK)),
        _resident(vecs.shape),
        pl.BlockSpec((tm, D_MODEL), lambda s: (jnp.minimum(s, n_tiles - 1), 0)),
    ] + [hbm] * len(weights)
    scratch_shapes = [pltpu.VMEM(w.shape, bf16) for w in weights] + [
        pltpu.VMEM((STAGE_SLOTS, D_MODEL, WEIGHT_CHUNK), f32),
        pltpu.SemaphoreType.DMA((STAGE_SLOTS,)),
        pltpu.VMEM((2, N_Q_HEADS, BLOCK, 2 * BLOCK), f32),
        pltpu.VMEM((tm, D_MODEL), bf16),
        pltpu.VMEM((tm + SUBLANES, D_MODEL), f32),
        pltpu.VMEM((tm, D_MODEL), bf16),
        pltpu.VMEM((tm, D_MODEL), f32),
        pltpu.VMEM((tm, D_MODEL), bf16),
        pltpu.VMEM((tm, D_MODEL), f32),
        pltpu.VMEM((2 * N_KV_HEADS, tm + BLOCK, LANES), bf16),
        pltpu.VMEM((2 * N_KV_HEADS, tm + BLOCK, LANES), bf16),
        pltpu.VMEM((tm, D_MODEL), bf16),
        pltpu.VMEM((SCORE_SLOTS, HEADS_PER_VREG, BLOCK, 2 * BLOCK), f32),
        pltpu.VMEM((4, tm, COL_CHUNK), f32),
    ]
    out = pl.pallas_call(
        functools.partial(_trunk_kernel, tiles_per_seq=tiles_per_seq),
        grid=(n_tiles + 1,),
        in_specs=in_specs,
        out_specs=pl.BlockSpec((tm, D_MODEL), lambda s: (jnp.maximum(s - 1, 0), 0)),
        out_shape=jax.ShapeDtypeStruct((bsz * seq, D_MODEL), f32),
        scratch_shapes=scratch_shapes,
        compiler_params=pltpu.CompilerParams(
            dimension_semantics=("arbitrary",),
            vmem_limit_bytes=VMEM_LIMIT_BYTES),
        name="yoco_trunk",
    )(rel_bias.astype(f32), b_sinks.reshape(1, N_Q_HEADS).astype(f32),
      jnp.asarray(_bucket_table()), vecs, x.reshape(bsz * seq, D_MODEL), *weights)
    return out.reshape(bsz, seq, D_MODEL)
```

```python
import functools
import math

import jax
import jax.numpy as jnp
import numpy as np
from jax.experimental import pallas as pl
from jax.experimental.pallas import tpu as pltpu

D_MODEL = 1024
CONV_K = 3
HEAD_DIM = 64
N_Q_HEADS = D_MODEL // HEAD_DIM
N_KV_HEADS = 2
GROUP = N_Q_HEADS // N_KV_HEADS
KV_WIDTH = N_KV_HEADS * HEAD_DIM
BLOCK = 128
N_BUCKETS = 32
MAX_DISTANCE = 128
EPS = 1e-6
NEG_INF = -1e30

LANES = 128
SUBLANES = 8
HEADS_PER_VREG = LANES // HEAD_DIM
N_PAIRS = N_Q_HEADS // HEADS_PER_VREG
PAIRS_PER_KV = GROUP // HEADS_PER_VREG
ROW_TILE = 256
COL_CHUNK = 256
GATE_SLOT = (2, 0, 1, 3)
TAIL_GROUPS = 1
SCORE_SLOTS = 3
WEIGHT_CHUNK = 512
STAGE_SLOTS = 3
VMEM_LIMIT_BYTES = 56 * 1024 * 1024


def _bucket_table():
    q_loc = np.arange(BLOCK, dtype=np.int32)[:, None]
    s_loc = np.arange(2 * BLOCK, dtype=np.int32)[None, :]
    dist = q_loc + BLOCK - s_loc
    in_window = (dist >= 0) & (dist < BLOCK)
    d = np.maximum(dist, 0)
    max_exact = N_BUCKETS // 2
    df = np.maximum(d, 1).astype(np.float32)
    large = max_exact + (
        np.log(df / np.float32(max_exact)) / np.float32(math.log(MAX_DISTANCE / max_exact))
        * np.float32(N_BUCKETS - max_exact)).astype(np.int32)
    large = np.minimum(large, N_BUCKETS - 1)
    bucket = np.where(d < max_exact, d, large)
    return np.where(in_window, bucket, -1).astype(np.int32)


def _rms_scale(x):
    return x * jax.lax.rsqrt(jnp.mean(x * x, axis=-1, keepdims=True) + EPS)


def _trunk_kernel(rel_bias_ref, sinks_ref, bucket_ref, vecs_ref, x_ref,
                  w_in_hbm, w_out_a_hbm, w_kv_hbm, w_in_b_hbm, w_out_b_hbm,
                  out_ref,
                  w_in_ref, w_out_a_ref, w_kv_ref, w_in_b_ref, w_out_b_ref, stage_scr, stage_sem,
                  bias_scr, hn_scr, v_scr, y_scr, h1_scr, q_scr, z_scr, kz_scr, vz_scr, g_scr,
                  s_scr, gate_scr,
                  *, tiles_per_seq):
    step = pl.program_id(0)
    tm = x_ref.shape[0]
    f32 = jnp.float32
    bf16 = jnp.bfloat16
    a_pre_ref, a_post_ref, kv_norm_ref, b_pre_ref, b_post_ref = (
        vecs_ref.at[i:i + 1, :] for i in range(5))
    conv_w_ref = vecs_ref.at[5:5 + CONV_K, :]

    @pl.when(step == 0)
    def _init():
        bucket = bucket_ref[...]
        has_prev = jax.lax.broadcasted_iota(jnp.int32, bucket.shape, 1) >= BLOCK

        def build_bias(h):
            acc = jnp.full(bucket.shape, NEG_INF, f32)
            for b in range(N_BUCKETS):
                acc = jnp.where(bucket == b, rel_bias_ref[b, h], acc)
            bias_scr[0, h] = acc
            bias_scr[1, h] = jnp.where(has_prev, acc, NEG_INF)

        chunks = [(src, c0, min(WEIGHT_CHUNK, src.shape[1]), dst)
                  for src, dst in ((w_in_hbm, w_in_ref), (w_out_a_hbm, w_out_a_ref),
                                   (w_kv_hbm, w_kv_ref), (w_in_b_hbm, w_in_b_ref),
                                   (w_out_b_hbm, w_out_b_ref))
                  for c0 in range(0, src.shape[1], WEIGHT_CHUNK)]

        def chunk_copy(i):
            src, c0, width, _ = chunks[i]
            slot = i % STAGE_SLOTS
            return pltpu.make_async_copy(src.at[:, c0:c0 + width],
                                         stage_scr.at[slot, :, 0:width], stage_sem.at[slot])

        def gate_major_col(col):
            gate, rest = divmod(col, D_MODEL)
            chunk, lane = divmod(rest, COL_CHUNK)
            return (chunk * 4 + GATE_SLOT[gate]) * COL_CHUNK + lane

        for i in range(min(STAGE_SLOTS, len(chunks))):
            chunk_copy(i).start()
        for i, (_, c0, width, dst) in enumerate(chunks):
            if i < N_Q_HEADS:
                build_bias(i)
            chunk_copy(i).wait()
            piece = COL_CHUNK if dst is w_in_ref else width
            for off in range(0, width, piece):
                d0 = gate_major_col(c0 + off) if dst is w_in_ref else c0 + off
                dst[:, d0:d0 + piece] = stage_scr[i % STAGE_SLOTS, :, off:off + piece].astype(bf16)
            if i + STAGE_SLOTS < len(chunks):
                chunk_copy(i + STAGE_SLOTS).start()
        for h in range(len(chunks), N_Q_HEADS):
            build_bias(h)
        v_scr[...] = jnp.zeros(v_scr.shape, f32)
        h1_scr[...] = jnp.zeros(h1_scr.shape, f32)
        q_scr[...] = jnp.zeros(q_scr.shape, bf16)
        z_scr[...] = jnp.zeros(z_scr.shape, f32)
        kz_scr[...] = jnp.zeros(kz_scr.shape, bf16)
        vz_scr[...] = jnp.zeros(vz_scr.shape, bf16)

    n_blocks = (tm // BLOCK) * N_PAIRS
    width = D_MODEL
    n_gate_dots = 2 * (width // COL_CHUNK)
    blocks_per_round, rem = divmod(n_blocks, n_gate_dots)
    assert rem == 0 and blocks_per_round >= 1

    b_seq_start = jax.lax.rem(step + (tiles_per_seq - 1), tiles_per_seq) == 0

    def block_slices(k):
        j, pair = divmod(k, N_PAIRS)
        return (slice(j * BLOCK, (j + 1) * BLOCK), slice(j * BLOCK, (j + 2) * BLOCK),
                slice(pair * LANES, (pair + 1) * LANES), j, pair)

    def b_scores(k):
        rows, key_rows, lanes, _, pair = block_slices(k)
        q_pair = q_scr[rows, lanes]
        for parity in range(HEADS_PER_VREG):
            s_scr[k % SCORE_SLOTS, parity] = jax.lax.dot_general(
                q_pair, kz_scr[(pair // PAIRS_PER_KV) * HEADS_PER_VREG + parity, key_rows, :],
                (((1,), (1,)), ((), ())), preferred_element_type=f32)

    def b_finish(k):
        rows, key_rows, lanes, j, pair = block_slices(k)
        first_block = jnp.where(b_seq_start, 1, 0) if j == 0 else 0
        o_pair = None
        for parity in range(HEADS_PER_VREG):
            head = pair * HEADS_PER_VREG + parity
            slot = (pair // PAIRS_PER_KV) * HEADS_PER_VREG + parity
            s = s_scr[k % SCORE_SLOTS, parity] + bias_scr[first_block, head]
            sink = sinks_ref[0, head]
            m = jnp.maximum(jnp.max(s, axis=-1, keepdims=True), sink)
            p = jnp.exp(s - m)
            denom = jnp.sum(p, axis=-1, keepdims=True) + jnp.exp(sink - m)
            pv = jnp.dot(p.astype(bf16), vz_scr[slot, key_rows, :], preferred_element_type=f32)
            pv = pv * (1.0 / denom)
            o_pair = pv if o_pair is None else o_pair + pv
        zg = z_scr[rows, lanes]
        g_scr[rows, lanes] = (o_pair * (zg * jax.nn.sigmoid(zg))).astype(bf16)

    a_seq_start = jax.lax.rem(step, tiles_per_seq) == 0

    def a_gate_dot(i):
        chunk, half = divmod(i, 2)
        cols = slice(chunk * COL_CHUNK, (chunk + 1) * COL_CHUNK)
        w0 = (chunk * 4 + 2 * half) * COL_CHUNK
        pair = jnp.dot(hn_scr[...], w_in_ref[:, w0:w0 + 2 * COL_CHUNK], preferred_element_type=f32)
        first, second = pair[:, 0:COL_CHUNK], pair[:, COL_CHUNK:2 * COL_CHUNK]
        if half == 0:
            v_scr[SUBLANES:SUBLANES + tm, cols] = first * second
            gate_scr[...] = (conv_w_ref[0:1, cols] * v_scr[SUBLANES - 2:SUBLANES - 2 + tm, cols]
                             + conv_w_ref[1:2, cols] * v_scr[SUBLANES - 1:SUBLANES - 1 + tm, cols]
                             + conv_w_ref[2:3, cols] * v_scr[SUBLANES:SUBLANES + tm, cols])
        else:
            y_scr[:, cols] = (first * gate_scr[...] * (second * jax.nn.sigmoid(second))).astype(bf16)

    out_ref[...] = h1_scr[...]
    v_scr[0:SUBLANES, :] = jnp.where(a_seq_start, 0.0, v_scr[tm:tm + SUBLANES, :])
    hn_scr[...] = (_rms_scale(x_ref[...]) * a_pre_ref[...]).astype(bf16)
    for k in range(SCORE_SLOTS - 1):
        b_scores(k)
    for r in range(n_gate_dots):
        a_gate_dot(r)
        for k in range(r * blocks_per_round, (r + 1) * blocks_per_round):
            b_finish(k)
            if k + SCORE_SLOTS - 1 < n_blocks:
                b_scores(k + SCORE_SLOTS - 1)

    groups = [slice(i * tm // TAIL_GROUPS, (i + 1) * tm // TAIL_GROUPS) for i in range(TAIL_GROUPS)]
    y = [jnp.dot(y_scr[r, :], w_out_a_ref[...], preferred_element_type=f32) for r in groups]
    y2 = [jnp.dot(g_scr[r, :], w_out_b_ref[...], preferred_element_type=f32) for r in groups]
    kz_scr[:, 0:BLOCK, :] = jnp.where(
        a_seq_start, 0.0, kz_scr[:, tm:tm + BLOCK, :].astype(f32)).astype(bf16)
    vz_scr[:, 0:BLOCK, :] = jnp.where(
        a_seq_start, 0.0, vz_scr[:, tm:tm + BLOCK, :].astype(f32)).astype(bf16)
    qn = []
    for r, y_r in zip(groups, y):
        rows = r.stop - r.start
        h1 = x_ref[r, :] + _rms_scale(y_r) * a_post_ref[...]
        h1_scr[r, :] = h1
        h1n = _rms_scale(h1)
        kv = jnp.dot((h1n * kv_norm_ref[...]).astype(bf16), w_kv_ref[...],
                     preferred_element_type=f32)
        low_half = jax.lax.broadcasted_iota(jnp.int32, (rows, LANES), 1) < HEAD_DIM
        key_rows = slice(BLOCK + r.start, BLOCK + r.stop)
        for src, dst in ((kv[:, 0:KV_WIDTH], kz_scr), (kv[:, KV_WIDTH:2 * KV_WIDTH], vz_scr)):
            swapped = pltpu.roll(src, HEAD_DIM, axis=1)
            dst[0, key_rows, :] = jnp.where(low_half, src, 0.0).astype(bf16)
            dst[1, key_rows, :] = jnp.where(low_half, 0.0, swapped).astype(bf16)
            dst[2, key_rows, :] = jnp.where(low_half, swapped, 0.0).astype(bf16)
            dst[3, key_rows, :] = jnp.where(low_half, 0.0, src).astype(bf16)
        qn.append((h1n * b_pre_ref[...]).astype(bf16))
        q = jnp.dot(qn[-1], w_in_b_ref[:, 0:D_MODEL], preferred_element_type=f32)
        q_scr[r, :] = (q * (HEAD_DIM ** -0.5)).astype(bf16)
    for r, y2_r, qn_r in zip(groups, y2, qn):
        out_ref[r, :] = out_ref[r, :] + _rms_scale(y2_r) * b_post_ref[...]
        z_scr[r, :] = jnp.dot(qn_r, w_in_b_ref[:, D_MODEL:2 * D_MODEL], preferred_element_type=f32)


def _resident(shape):
    return pl.BlockSpec(shape, lambda s: (0,) * len(shape), pipeline_mode=pl.Buffered(1))


@jax.jit
def kernel(x, a_pre_norm, a_w_in, a_conv_w, a_w_out, a_post_norm, kv_norm, w_kv, rel_bias,
           b_pre_norm, b_w_in, b_sinks, b_w_out, b_post_norm):
    bsz, seq, d_model = x.shape
    assert d_model == D_MODEL and seq % ROW_TILE == 0
    assert a_w_in.shape[0] == 1 and b_w_in.shape[0] == 1
    tm = ROW_TILE
    tiles_per_seq = seq // tm
    n_tiles = bsz * tiles_per_seq
    bf16 = jnp.bfloat16
    f32 = jnp.float32
    smem = pl.BlockSpec(memory_space=pltpu.SMEM)
    hbm = pl.BlockSpec(memory_space=pl.ANY)
    weights = (a_w_in[0], a_w_out[0], w_kv, b_w_in[0], b_w_out[0])
    assert all(w.dtype == f32 and w.shape[0] == D_MODEL for w in weights)
    vecs = jnp.concatenate(
        [g.reshape(1, D_MODEL) for g in (a_pre_norm, a_post_norm, kv_norm, b_pre_norm, b_post_norm)]
        + [a_conv_w[0]], axis=0).astype(f32)

    in_specs = [
        smem,
        smem,
        _resident((BLOCK, 2 * BLOCK)),
        _resident(vecs.shape),
        pl.BlockSpec((tm, D_MODEL), lambda s: (jnp.minimum(s, n_tiles - 1), 0)),
    ] + [hbm] * len(weights)
    scratch_shapes = [pltpu.VMEM(w.shape, bf16) for w in weights] + [
        pltpu.VMEM((STAGE_SLOTS, D_MODEL, WEIGHT_CHUNK), f32),
        pltpu.SemaphoreType.DMA((STAGE_SLOTS,)),
        pltpu.VMEM((2, N_Q_HEADS, BLOCK, 2 * BLOCK), f32),
        pltpu.VMEM((tm, D_MODEL), bf16),
        pltpu.VMEM((tm + SUBLANES, D_MODEL), f32),
        pltpu.VMEM((tm, D_MODEL), bf16),
        pltpu.VMEM((tm, D_MODEL), f32),
        pltpu.VMEM((tm, D_MODEL), bf16),
        pltpu.VMEM((tm, D_MODEL), f32),
        pltpu.VMEM((2 * N_KV_HEADS, tm + BLOCK, LANES), bf16),
        pltpu.VMEM((2 * N_KV_HEADS, tm + BLOCK, LANES), bf16),
        pltpu.VMEM((tm, D_MODEL), bf16),
        pltpu.VMEM((SCORE_SLOTS, HEADS_PER_VREG, BLOCK, 2 * BLOCK), f32),
        pltpu.VMEM((tm, COL_CHUNK), f32),
    ]
    out = pl.pallas_call(
        functools.partial(_trunk_kernel, tiles_per_seq=tiles_per_seq),
        grid=(n_tiles + 1,),
        in_specs=in_specs,
        out_specs=pl.BlockSpec((tm, D_MODEL), lambda s: (jnp.maximum(s - 1, 0), 0)),
        out_shape=jax.ShapeDtypeStruct((bsz * seq, D_MODEL), f32),
        scratch_shapes=scratch_shapes,
        compiler_params=pltpu.CompilerParams(
            dimension_semantics=("arbitrary",),
            vmem_limit_bytes=VMEM_LIMIT_BYTES),
        name="yoco_trunk",
    )(rel_bias.astype(f32), b_sinks.reshape(1, N_Q_HEADS).astype(f32),
      jnp.asarray(_bucket_table()), vecs, x.reshape(bsz * seq, D_MODEL), *weights)
    return out.reshape(bsz, seq, D_MODEL)
```

```python
import functools
import math

import jax
import jax.numpy as jnp
import numpy as np
from jax.experimental import pallas as pl
from jax.experimental.pallas import tpu as pltpu

D_MODEL = 1024
CONV_K = 3
HEAD_DIM = 64
N_Q_HEADS = D_MODEL // HEAD_DIM
N_KV_HEADS = 2
GROUP = N_Q_HEADS // N_KV_HEADS
KV_WIDTH = N_KV_HEADS * HEAD_DIM
BLOCK = 128
N_BUCKETS = 32
MAX_DISTANCE = 128
EPS = 1e-6
NEG_INF = -1e30

LANES = 128
SUBLANES = 8
HEADS_PER_VREG = LANES // HEAD_DIM
N_PAIRS = N_Q_HEADS // HEADS_PER_VREG
PAIRS_PER_KV = GROUP // HEADS_PER_VREG
ROW_TILE = 256
TILES_PER_STEP = 2
COL_CHUNK = 256
GATE_SLOT = (2, 0, 1, 3)
SCORE_SLOTS = 3
WEIGHT_CHUNK = 512
STAGE_SLOTS = 3
PIPELINE_BUFFERS = 2
SPILL_ALLOWANCE_BYTES = 12 * 1024 * 1024


def _bucket_table():
    q_loc = np.arange(BLOCK, dtype=np.int32)[:, None]
    s_loc = np.arange(2 * BLOCK, dtype=np.int32)[None, :]
    dist = q_loc + BLOCK - s_loc
    in_window = (dist >= 0) & (dist < BLOCK)
    d = np.maximum(dist, 0)
    max_exact = N_BUCKETS // 2
    df = np.maximum(d, 1).astype(np.float32)
    large = max_exact + (
        np.log(df / np.float32(max_exact)) / np.float32(math.log(MAX_DISTANCE / max_exact))
        * np.float32(N_BUCKETS - max_exact)).astype(np.int32)
    large = np.minimum(large, N_BUCKETS - 1)
    bucket = np.where(d < max_exact, d, large)
    return np.where(in_window, bucket, -1).astype(np.int32)


def _rms_scale(x):
    return x * jax.lax.rsqrt(jnp.mean(x * x, axis=-1, keepdims=True) + EPS)


def _trunk_kernel(rel_bias_ref, sinks_ref, bucket_ref, vecs_ref, x_first_ref, x_odd_ref, x_even_ref,
                  w_in_hbm, w_out_a_hbm, w_kv_hbm, w_in_b_hbm, w_out_b_hbm,
                  out_ref,
                  w_in_ref, w_out_a_ref, w_kv_ref, w_in_b_ref, w_out_b_ref, stage_scr, stage_sem,
                  bias_scr, hn_scr, v_scr, y_scr, h1_scr, q_scr, z_scr, kz_scr, vz_scr, g_scr,
                  s_scr, gate_scr,
                  *, tiles_per_seq):
    step = pl.program_id(0)
    tm = ROW_TILE
    f32 = jnp.float32
    bf16 = jnp.bfloat16
    a_pre_ref, a_post_ref, kv_norm_ref, b_pre_ref, b_post_ref = (
        vecs_ref.at[i:i + 1, :] for i in range(5))
    conv_w_ref = vecs_ref.at[5:5 + CONV_K, :]

    n_blocks = (tm // BLOCK) * N_PAIRS
    n_gate_dots = 2 * (D_MODEL // COL_CHUNK)
    blocks_per_round, rem = divmod(n_blocks, n_gate_dots)
    assert rem == 0 and blocks_per_round >= 1

    def block_slices(k):
        j, pair = divmod(k, N_PAIRS)
        return (slice(j * BLOCK, (j + 1) * BLOCK), slice(j * BLOCK, (j + 2) * BLOCK),
                slice(pair * LANES, (pair + 1) * LANES), j, pair)

    def input_norm(x_ref, slot):
        hn_scr[slot] = (_rms_scale(x_ref[...]) * a_pre_ref[...]).astype(bf16)

    def half_step(attn_set, seq_start_b, x_ref, conv_set, seq_start_a, out_rows, mid_hook=None):
        def b_scores(k):
            rows, key_rows, lanes, _, pair = block_slices(k)
            q_pair = q_scr[attn_set, rows, lanes]
            for parity in range(HEADS_PER_VREG):
                s_scr[k % SCORE_SLOTS, parity] = jax.lax.dot_general(
                    q_pair,
                    kz_scr[attn_set, (pair // PAIRS_PER_KV) * HEADS_PER_VREG + parity, key_rows, :],
                    (((1,), (1,)), ((), ())), preferred_element_type=f32)

        def b_finish(k):
            rows, key_rows, lanes, j, pair = block_slices(k)
            first_block = jnp.where(seq_start_b, 1, 0) if (j == 0 and seq_start_b is not False) else 0
            o_pair = None
            for parity in range(HEADS_PER_VREG):
                head = pair * HEADS_PER_VREG + parity
                slot = (pair // PAIRS_PER_KV) * HEADS_PER_VREG + parity
                s = s_scr[k % SCORE_SLOTS, parity] + bias_scr[first_block, head]
                sink = sinks_ref[0, head]
                m = jnp.maximum(jnp.max(s, axis=-1, keepdims=True), sink)
                p = jnp.exp(s - m)
                denom = jnp.sum(p, axis=-1, keepdims=True) + jnp.exp(sink - m)
                pv = jnp.dot(p.astype(bf16), vz_scr[attn_set, slot, key_rows, :],
                             preferred_element_type=f32)
                pv = pv * (1.0 / denom)
                o_pair = pv if o_pair is None else o_pair + pv
            zg = z_scr[attn_set, rows, lanes]
            g_scr[rows, lanes] = (o_pair * (zg * jax.nn.sigmoid(zg))).astype(bf16)

        def a_gate_dot(i):
            chunk, half = divmod(i, 2)
            cols = slice(chunk * COL_CHUNK, (chunk + 1) * COL_CHUNK)
            w0 = (chunk * 4 + 2 * half) * COL_CHUNK
            pair = jnp.dot(hn_scr[conv_set], w_in_ref[:, w0:w0 + 2 * COL_CHUNK],
                           preferred_element_type=f32)
            first, second = pair[:, 0:COL_CHUNK], pair[:, COL_CHUNK:2 * COL_CHUNK]
            if half == 0:
                v_scr[SUBLANES:SUBLANES + tm, cols] = first * second
                gate_scr[...] = (
                    conv_w_ref[0:1, cols] * v_scr[SUBLANES - 2:SUBLANES - 2 + tm, cols]
                    + conv_w_ref[1:2, cols] * v_scr[SUBLANES - 1:SUBLANES - 1 + tm, cols]
                    + conv_w_ref[2:3, cols] * v_scr[SUBLANES:SUBLANES + tm, cols])
            else:
                y_scr[:, cols] = (first * gate_scr[...]
                                  * (second * jax.nn.sigmoid(second))).astype(bf16)

        if seq_start_a is True:
            v_scr[0:SUBLANES, :] = jnp.zeros((SUBLANES, D_MODEL), f32)
        elif seq_start_a is False:
            v_scr[0:SUBLANES, :] = v_scr[tm:tm + SUBLANES, :]
        else:
            v_scr[0:SUBLANES, :] = jnp.where(seq_start_a, 0.0, v_scr[tm:tm + SUBLANES, :])
        if attn_set is not None:
            for k in range(SCORE_SLOTS - 1):
                b_scores(k)
        for r in range(n_gate_dots):
            a_gate_dot(r)
            if attn_set is not None:
                for k in range(r * blocks_per_round, (r + 1) * blocks_per_round):
                    b_finish(k)
                    if k + SCORE_SLOTS - 1 < n_blocks:
                        b_scores(k + SCORE_SLOTS - 1)
            if mid_hook is not None and r == n_gate_dots // 2:
                mid_hook()

        y = jnp.dot(y_scr[...], w_out_a_ref[...], preferred_element_type=f32)
        if attn_set is not None:
            y2 = jnp.dot(g_scr[...], w_out_b_ref[...], preferred_element_type=f32)
        for scr in (kz_scr, vz_scr):
            if seq_start_a is True:
                scr[conv_set, :, 0:BLOCK, :] = jnp.zeros((2 * N_KV_HEADS, BLOCK, LANES), bf16)
            elif seq_start_a is False:
                scr[conv_set, :, 0:BLOCK, :] = scr[1 - conv_set, :, tm:tm + BLOCK, :]
            else:
                scr[conv_set, :, 0:BLOCK, :] = jnp.where(
                    seq_start_a, 0.0, scr[1 - conv_set, :, tm:tm + BLOCK, :].astype(f32)).astype(bf16)
        h1 = x_ref[...] + _rms_scale(y) * a_post_ref[...]
        h1_scr[conv_set] = h1
        h1n = _rms_scale(h1)
        kv = jnp.dot((h1n * kv_norm_ref[...]).astype(bf16), w_kv_ref[...],
                     preferred_element_type=f32)
        low_half = jax.lax.broadcasted_iota(jnp.int32, (tm, LANES), 1) < HEAD_DIM
        for src, dst in ((kv[:, 0:KV_WIDTH], kz_scr), (kv[:, KV_WIDTH:2 * KV_WIDTH], vz_scr)):
            swapped = pltpu.roll(src, HEAD_DIM, axis=1)
            dst[conv_set, 0, BLOCK:BLOCK + tm, :] = jnp.where(low_half, src, 0.0).astype(bf16)
            dst[conv_set, 1, BLOCK:BLOCK + tm, :] = jnp.where(low_half, 0.0, swapped).astype(bf16)
            dst[conv_set, 2, BLOCK:BLOCK + tm, :] = jnp.where(low_half, swapped, 0.0).astype(bf16)
            dst[conv_set, 3, BLOCK:BLOCK + tm, :] = jnp.where(low_half, 0.0, src).astype(bf16)
        qn = (h1n * b_pre_ref[...]).astype(bf16)
        q = jnp.dot(qn, w_in_b_ref[:, 0:D_MODEL], preferred_element_type=f32)
        q_scr[conv_set] = (q * (HEAD_DIM ** -0.5)).astype(bf16)
        if attn_set is not None:
            out_ref[out_rows, :] = h1_scr[attn_set] + _rms_scale(y2) * b_post_ref[...]
        z_scr[conv_set] = jnp.dot(qn, w_in_b_ref[:, D_MODEL:2 * D_MODEL],
                                  preferred_element_type=f32)

    @pl.when(step == 0)
    def _init():
        bucket = bucket_ref[...]
        has_prev = jax.lax.broadcasted_iota(jnp.int32, bucket.shape, 1) >= BLOCK

        def build_bias(h):
            acc = jnp.full(bucket.shape, NEG_INF, f32)
            for b in range(N_BUCKETS):
                acc = jnp.where(bucket == b, rel_bias_ref[b, h], acc)
            bias_scr[0, h] = acc
            bias_scr[1, h] = jnp.where(has_prev, acc, NEG_INF)

        chunks = [(src, c0, min(WEIGHT_CHUNK, src.shape[1]), dst)
                  for src, dst in ((w_in_hbm, w_in_ref), (w_out_a_hbm, w_out_a_ref),
                                   (w_kv_hbm, w_kv_ref), (w_in_b_hbm, w_in_b_ref),
                                   (w_out_b_hbm, w_out_b_ref))
                  for c0 in range(0, src.shape[1], WEIGHT_CHUNK)]

        def chunk_copy(i):
            src, c0, width, _ = chunks[i]
            slot = i % STAGE_SLOTS
            return pltpu.make_async_copy(src.at[:, c0:c0 + width],
                                         stage_scr.at[slot, :, 0:width], stage_sem.at[slot])

        def gate_major_col(col):
            gate, rest = divmod(col, D_MODEL)
            chunk, lane = divmod(rest, COL_CHUNK)
            return (chunk * 4 + GATE_SLOT[gate]) * COL_CHUNK + lane

        for i in range(min(STAGE_SLOTS, len(chunks))):
            chunk_copy(i).start()
        for i, (_, c0, width, dst) in enumerate(chunks):
            if i < N_Q_HEADS:
                build_bias(i)
            chunk_copy(i).wait()
            piece = COL_CHUNK if dst is w_in_ref else width
            for off in range(0, width, piece):
                d0 = gate_major_col(c0 + off) if dst is w_in_ref else c0 + off
                dst[:, d0:d0 + piece] = stage_scr[i % STAGE_SLOTS, :, off:off + piece].astype(bf16)
            if i + STAGE_SLOTS < len(chunks):
                chunk_copy(i + STAGE_SLOTS).start()
        for h in range(len(chunks), N_Q_HEADS):
            build_bias(h)
        input_norm(x_first_ref, 0)
        half_step(None, False, x_first_ref, 0, True, None)

    even_seq_start = jax.lax.rem(TILES_PER_STEP * step + TILES_PER_STEP, tiles_per_seq) == 0
    attn_seq_start = jax.lax.rem(TILES_PER_STEP * step, tiles_per_seq) == 0
    input_norm(x_odd_ref, 1)
    half_step(0, attn_seq_start, x_odd_ref, 1, False, slice(0, tm),
              mid_hook=lambda: input_norm(x_even_ref, 0))
    half_step(1, False, x_even_ref, 0, even_seq_start, slice(tm, 2 * tm))


def _resident(shape):
    return pl.BlockSpec(shape, lambda s: (0,) * len(shape), pipeline_mode=pl.Buffered(1))


def _nbytes(shape, dtype):
    return math.prod(shape) * jnp.dtype(dtype).itemsize


@jax.jit
def kernel(x, a_pre_norm, a_w_in, a_conv_w, a_w_out, a_post_norm, kv_norm, w_kv, rel_bias,
           b_pre_norm, b_w_in, b_sinks, b_w_out, b_post_norm):
    bsz, seq, d_model = x.shape
    tm = ROW_TILE
    assert d_model == D_MODEL and seq % (tm * TILES_PER_STEP) == 0 and TILES_PER_STEP == 2
    assert a_w_in.shape[0] == 1 and b_w_in.shape[0] == 1
    tiles_per_seq = seq // tm
    n_tiles = bsz * tiles_per_seq
    bf16 = jnp.bfloat16
    f32 = jnp.float32
    smem = pl.BlockSpec(memory_space=pltpu.SMEM)
    hbm = pl.BlockSpec(memory_space=pl.ANY)
    weights = (a_w_in[0], a_w_out[0], w_kv, b_w_in[0], b_w_out[0])
    assert all(w.dtype == f32 and w.shape[0] == D_MODEL for w in weights)
    vecs = jnp.concatenate(
        [g.reshape(1, D_MODEL) for g in (a_pre_norm, a_post_norm, kv_norm, b_pre_norm, b_post_norm)]
        + [a_conv_w[0]], axis=0).astype(f32)
    bucket = _bucket_table()
    x2d = x.reshape(bsz * seq, D_MODEL)

    in_specs = [
        smem,
        smem,
        _resident(bucket.shape),
        _resident(vecs.shape),
        _resident((tm, D_MODEL)),
        pl.BlockSpec((tm, D_MODEL), lambda s: (2 * s + 1, 0)),
        pl.BlockSpec((tm, D_MODEL), lambda s: (jnp.minimum(2 * s + 2, n_tiles - 1), 0)),
    ] + [hbm] * len(weights)
    per_set = 2
    vmem_scratch = [(w.shape, bf16) for w in weights] + [
        ((STAGE_SLOTS, D_MODEL, WEIGHT_CHUNK), f32),
        ((2, N_Q_HEADS, BLOCK, 2 * BLOCK), f32),
        ((per_set, tm, D_MODEL), bf16),
        ((tm + SUBLANES, D_MODEL), f32),
        ((tm, D_MODEL), bf16),
        ((per_set, tm, D_MODEL), f32),
        ((per_set, tm, D_MODEL), bf16),
        ((per_set, tm, D_MODEL), f32),
        ((per_set, 2 * N_KV_HEADS, tm + BLOCK, LANES), bf16),
        ((per_set, 2 * N_KV_HEADS, tm + BLOCK, LANES), bf16),
        ((tm, D_MODEL), bf16),
        ((SCORE_SLOTS, HEADS_PER_VREG, BLOCK, 2 * BLOCK), f32),
        ((tm, COL_CHUNK), f32),
    ]
    n_weights = len(weights)
    scratch_shapes = ([pltpu.VMEM(*sd) for sd in vmem_scratch[:n_weights + 1]]
                      + [pltpu.SemaphoreType.DMA((STAGE_SLOTS,))]
                      + [pltpu.VMEM(*sd) for sd in vmem_scratch[n_weights + 1:]])
    vmem_limit = (sum(_nbytes(*sd) for sd in vmem_scratch)
                  + PIPELINE_BUFFERS * (2 + TILES_PER_STEP) * _nbytes((tm, D_MODEL), f32)
                  + _nbytes((tm, D_MODEL), f32)
                  + _nbytes(bucket.shape, bucket.dtype) + _nbytes(vecs.shape, f32)
                  + SPILL_ALLOWANCE_BYTES)
    out = pl.pallas_call(
        functools.partial(_trunk_kernel, tiles_per_seq=tiles_per_seq),
        grid=(n_tiles // TILES_PER_STEP,),
        in_specs=in_specs,
        out_specs=pl.BlockSpec((TILES_PER_STEP * tm, D_MODEL), lambda s: (s, 0)),
        out_shape=jax.ShapeDtypeStruct((bsz * seq, D_MODEL), f32),
        scratch_shapes=scratch_shapes,
        compiler_params=pltpu.CompilerParams(
            dimension_semantics=("arbitrary",),
            vmem_limit_bytes=vmem_limit),
        name="yoco_trunk",
    )(rel_bias.astype(f32), b_sinks.reshape(1, N_Q_HEADS).astype(f32),
      jnp.asarray(bucket), vecs, x2d, x2d, x2d, *weights)
    return out.reshape(bsz, seq, D_MODEL)
```

```python
import functools
import math

import jax
import jax.numpy as jnp
import numpy as np
from jax.experimental import pallas as pl
from jax.experimental.pallas import tpu as pltpu

D_MODEL = 1024
CONV_K = 3
HEAD_DIM = 64
N_Q_HEADS = D_MODEL // HEAD_DIM
N_KV_HEADS = 2
GROUP = N_Q_HEADS // N_KV_HEADS
KV_WIDTH = N_KV_HEADS * HEAD_DIM
BLOCK = 128
N_BUCKETS = 32
MAX_DISTANCE = 128
EPS = 1e-6
NEG_INF = -1e30

LANES = 128
SUBLANES = 8
HEADS_PER_VREG = LANES // HEAD_DIM
N_PAIRS = N_Q_HEADS // HEADS_PER_VREG
PAIRS_PER_KV = GROUP // HEADS_PER_VREG
ROW_TILE = 256
TILES_PER_STEP = 2
COL_CHUNK = 256
GATE_SLOT = (2, 0, 1, 3)
SCORE_SLOTS = 3
WEIGHT_CHUNK = 512
STAGE_SLOTS = 3
PIPELINE_BUFFERS = 2
SPILL_ALLOWANCE_BYTES = 12 * 1024 * 1024


def _bucket_table():
    q_loc = np.arange(BLOCK, dtype=np.int32)[:, None]
    s_loc = np.arange(2 * BLOCK, dtype=np.int32)[None, :]
    dist = q_loc + BLOCK - s_loc
    in_window = (dist >= 0) & (dist < BLOCK)
    d = np.maximum(dist, 0)
    max_exact = N_BUCKETS // 2
    df = np.maximum(d, 1).astype(np.float32)
    large = max_exact + (
        np.log(df / np.float32(max_exact)) / np.float32(math.log(MAX_DISTANCE / max_exact))
        * np.float32(N_BUCKETS - max_exact)).astype(np.int32)
    large = np.minimum(large, N_BUCKETS - 1)
    bucket = np.where(d < max_exact, d, large)
    return np.where(in_window, bucket, -1).astype(np.int32)


def _rms_scale(x):
    return x * jax.lax.rsqrt(jnp.mean(x * x, axis=-1, keepdims=True) + EPS)


def _trunk_kernel(rel_bias_ref, sinks_ref, bucket_ref, vecs_ref, x_first_ref, x_odd_ref, x_even_ref,
                  w_in_hbm, w_out_a_hbm, w_kv_hbm, w_in_b_hbm, w_out_b_hbm,
                  out_ref,
                  w_in_ref, w_out_a_ref, w_kv_ref, w_in_b_ref, w_out_b_ref, stage_scr, stage_sem,
                  bias_scr, hn_scr, v_scr, y_scr, h1_scr, q_scr, z_scr, kz_scr, vz_scr, g_scr,
                  s_scr, gate_scr,
                  *, tiles_per_seq):
    step = pl.program_id(0)
    tm = ROW_TILE
    f32 = jnp.float32
    bf16 = jnp.bfloat16
    a_pre_ref, a_post_ref, kv_norm_ref, b_pre_ref, b_post_ref = (
        vecs_ref.at[i:i + 1, :] for i in range(5))
    conv_w_ref = vecs_ref.at[5:5 + CONV_K, :]

    n_blocks = (tm // BLOCK) * N_PAIRS
    n_gate_dots = 2 * (D_MODEL // COL_CHUNK)
    blocks_per_round, rem = divmod(n_blocks, n_gate_dots)
    assert rem == 0 and blocks_per_round >= 1

    def wcols(w_ref, c0, c1):
        return jnp.concatenate([w_ref[t] for t in range(c0 // COL_CHUNK, c1 // COL_CHUNK)], axis=1)

    def block_slices(k):
        j, pair = divmod(k, N_PAIRS)
        return (slice(j * BLOCK, (j + 1) * BLOCK), slice(j * BLOCK, (j + 2) * BLOCK),
                slice(pair * LANES, (pair + 1) * LANES), j, pair)

    def input_norm(x_ref, slot):
        hn_scr[slot] = (_rms_scale(x_ref[...]) * a_pre_ref[...]).astype(bf16)

    def half_step(attn_set, seq_start_b, x_ref, conv_set, seq_start_a, out_rows, mid_hook=None):
        def b_scores(k):
            rows, key_rows, lanes, _, pair = block_slices(k)
            q_pair = q_scr[attn_set, pair, rows, :]
            for parity in range(HEADS_PER_VREG):
                s_scr[k % SCORE_SLOTS, parity] = jax.lax.dot_general(
                    q_pair,
                    kz_scr[attn_set, (pair // PAIRS_PER_KV) * HEADS_PER_VREG + parity, key_rows, :],
                    (((1,), (1,)), ((), ())), preferred_element_type=f32)

        def b_finish(k):
            rows, key_rows, lanes, j, pair = block_slices(k)
            first_block = jnp.where(seq_start_b, 1, 0) if (j == 0 and seq_start_b is not False) else 0
            o_pair = None
            for parity in range(HEADS_PER_VREG):
                head = pair * HEADS_PER_VREG + parity
                slot = (pair // PAIRS_PER_KV) * HEADS_PER_VREG + parity
                s = s_scr[k % SCORE_SLOTS, parity] + bias_scr[first_block, head]
                sink = sinks_ref[0, head]
                m = jnp.maximum(jnp.max(s, axis=-1, keepdims=True), sink)
                p = jnp.exp(s - m)
                denom = jnp.sum(p, axis=-1, keepdims=True) + jnp.exp(sink - m)
                pv = jnp.dot(p.astype(bf16), vz_scr[attn_set, slot, key_rows, :],
                             preferred_element_type=f32)
                pv = pv * (1.0 / denom)
                o_pair = pv if o_pair is None else o_pair + pv
            zg = z_scr[attn_set, pair, rows, :]
            g_scr[pair, rows, :] = (o_pair * (zg * jax.nn.sigmoid(zg))).astype(bf16)

        def a_gate_dot(i):
            chunk, half = divmod(i, 2)
            cols = slice(chunk * COL_CHUNK, (chunk + 1) * COL_CHUNK)
            w0 = (chunk * 4 + 2 * half) * COL_CHUNK
            pair = jnp.dot(hn_scr[conv_set], wcols(w_in_ref, w0, w0 + 2 * COL_CHUNK),
                           preferred_element_type=f32)
            first, second = pair[:, 0:COL_CHUNK], pair[:, COL_CHUNK:2 * COL_CHUNK]
            if half == 0:
                v_scr[chunk, SUBLANES:SUBLANES + tm, :] = first * second
                gate_scr[...] = (
                    conv_w_ref[0:1, cols] * v_scr[chunk, SUBLANES - 2:SUBLANES - 2 + tm, :]
                    + conv_w_ref[1:2, cols] * v_scr[chunk, SUBLANES - 1:SUBLANES - 1 + tm, :]
                    + conv_w_ref[2:3, cols] * v_scr[chunk, SUBLANES:SUBLANES + tm, :])
            else:
                y_scr[chunk] = (first * gate_scr[...]
                                  * (second * jax.nn.sigmoid(second))).astype(bf16)

        if seq_start_a is True:
            v_scr[:, 0:SUBLANES, :] = jnp.zeros((D_MODEL // COL_CHUNK, SUBLANES, COL_CHUNK), f32)
        elif seq_start_a is False:
            v_scr[:, 0:SUBLANES, :] = v_scr[:, tm:tm + SUBLANES, :]
        else:
            v_scr[:, 0:SUBLANES, :] = jnp.where(seq_start_a, 0.0, v_scr[:, tm:tm + SUBLANES, :])
        if attn_set is not None:
            for k in range(SCORE_SLOTS - 1):
                b_scores(k)
        for r in range(n_gate_dots):
            a_gate_dot(r)
            if attn_set is not None:
                for k in range(r * blocks_per_round, (r + 1) * blocks_per_round):
                    b_finish(k)
                    if k + SCORE_SLOTS - 1 < n_blocks:
                        b_scores(k + SCORE_SLOTS - 1)
            if mid_hook is not None and r == n_gate_dots // 2:
                mid_hook()

        y_in = jnp.concatenate([y_scr[c] for c in range(D_MODEL // COL_CHUNK)], axis=1)
        y = jnp.dot(y_in, wcols(w_out_a_ref, 0, D_MODEL), preferred_element_type=f32)
        if attn_set is not None:
            g_in = jnp.concatenate([g_scr[p] for p in range(N_PAIRS)], axis=1)
            y2 = jnp.dot(g_in, wcols(w_out_b_ref, 0, D_MODEL), preferred_element_type=f32)
        for scr in (kz_scr, vz_scr):
            if seq_start_a is True:
                scr[conv_set, :, 0:BLOCK, :] = jnp.zeros((2 * N_KV_HEADS, BLOCK, LANES), bf16)
            elif seq_start_a is False:
                scr[conv_set, :, 0:BLOCK, :] = scr[1 - conv_set, :, tm:tm + BLOCK, :]
            else:
                scr[conv_set, :, 0:BLOCK, :] = jnp.where(
                    seq_start_a, 0.0, scr[1 - conv_set, :, tm:tm + BLOCK, :].astype(f32)).astype(bf16)
        h1 = x_ref[...] + _rms_scale(y) * a_post_ref[...]
        h1_scr[conv_set] = h1
        h1n = _rms_scale(h1)
        kv = jnp.dot((h1n * kv_norm_ref[...]).astype(bf16), wcols(w_kv_ref, 0, 2 * KV_WIDTH),
                     preferred_element_type=f32)
        low_half = jax.lax.broadcasted_iota(jnp.int32, (tm, LANES), 1) < HEAD_DIM
        for src, dst in ((kv[:, 0:KV_WIDTH], kz_scr), (kv[:, KV_WIDTH:2 * KV_WIDTH], vz_scr)):
            swapped = pltpu.roll(src, HEAD_DIM, axis=1)
            dst[conv_set, 0, BLOCK:BLOCK + tm, :] = jnp.where(low_half, src, 0.0).astype(bf16)
            dst[conv_set, 1, BLOCK:BLOCK + tm, :] = jnp.where(low_half, 0.0, swapped).astype(bf16)
            dst[conv_set, 2, BLOCK:BLOCK + tm, :] = jnp.where(low_half, swapped, 0.0).astype(bf16)
            dst[conv_set, 3, BLOCK:BLOCK + tm, :] = jnp.where(low_half, 0.0, src).astype(bf16)
        qn = (h1n * b_pre_ref[...]).astype(bf16)
        q = jnp.dot(qn, wcols(w_in_b_ref, 0, D_MODEL), preferred_element_type=f32)
        for pair in range(N_PAIRS):
            q_scr[conv_set, pair] = (q[:, pair * LANES:(pair + 1) * LANES]
                                     * (HEAD_DIM ** -0.5)).astype(bf16)
        if attn_set is not None:
            out_ref[out_rows, :] = h1_scr[attn_set] + _rms_scale(y2) * b_post_ref[...]
        z = jnp.dot(qn, wcols(w_in_b_ref, D_MODEL, 2 * D_MODEL), preferred_element_type=f32)
        for pair in range(N_PAIRS):
            z_scr[conv_set, pair] = z[:, pair * LANES:(pair + 1) * LANES]

    @pl.when(step == 0)
    def _init():
        bucket = bucket_ref[...]
        has_prev = jax.lax.broadcasted_iota(jnp.int32, bucket.shape, 1) >= BLOCK

        def build_bias(h):
            acc = jnp.full(bucket.shape, NEG_INF, f32)
            for b in range(N_BUCKETS):
                acc = jnp.where(bucket == b, rel_bias_ref[b, h], acc)
            bias_scr[0, h] = acc
            bias_scr[1, h] = jnp.where(has_prev, acc, NEG_INF)

        chunks = [(src, c0, min(WEIGHT_CHUNK, src.shape[1]), dst)
                  for src, dst in ((w_in_hbm, w_in_ref), (w_out_a_hbm, w_out_a_ref),
                                   (w_kv_hbm, w_kv_ref), (w_in_b_hbm, w_in_b_ref),
                                   (w_out_b_hbm, w_out_b_ref))
                  for c0 in range(0, src.shape[1], WEIGHT_CHUNK)]

        def chunk_copy(i):
            src, c0, width, _ = chunks[i]
            slot = i % STAGE_SLOTS
            return pltpu.make_async_copy(src.at[:, c0:c0 + width],
                                         stage_scr.at[slot, :, 0:width], stage_sem.at[slot])

        def gate_major_col(col):
            gate, rest = divmod(col, D_MODEL)
            chunk, lane = divmod(rest, COL_CHUNK)
            return (chunk * 4 + GATE_SLOT[gate]) * COL_CHUNK + lane

        for i in range(min(STAGE_SLOTS, len(chunks))):
            chunk_copy(i).start()
        for i, (_, c0, width, dst) in enumerate(chunks):
            if i < N_Q_HEADS:
                build_bias(i)
            chunk_copy(i).wait()
            for off in range(0, width, COL_CHUNK):
                d0 = gate_major_col(c0 + off) if dst is w_in_ref else c0 + off
                dst[d0 // COL_CHUNK] = stage_scr[i % STAGE_SLOTS, :, off:off + COL_CHUNK].astype(bf16)
            if i + STAGE_SLOTS < len(chunks):
                chunk_copy(i + STAGE_SLOTS).start()
        for h in range(len(chunks), N_Q_HEADS):
            build_bias(h)
        input_norm(x_first_ref, 0)
        half_step(None, False, x_first_ref, 0, True, None)

    even_seq_start = jax.lax.rem(TILES_PER_STEP * step + TILES_PER_STEP, tiles_per_seq) == 0
    attn_seq_start = jax.lax.rem(TILES_PER_STEP * step, tiles_per_seq) == 0
    input_norm(x_odd_ref, 1)
    half_step(0, attn_seq_start, x_odd_ref, 1, False, slice(0, tm),
              mid_hook=lambda: input_norm(x_even_ref, 0))
    half_step(1, False, x_even_ref, 0, even_seq_start, slice(tm, 2 * tm))


def _resident(shape):
    return pl.BlockSpec(shape, lambda s: (0,) * len(shape), pipeline_mode=pl.Buffered(1))


def _nbytes(shape, dtype):
    return math.prod(shape) * jnp.dtype(dtype).itemsize


@jax.jit
def kernel(x, a_pre_norm, a_w_in, a_conv_w, a_w_out, a_post_norm, kv_norm, w_kv, rel_bias,
           b_pre_norm, b_w_in, b_sinks, b_w_out, b_post_norm):
    bsz, seq, d_model = x.shape
    tm = ROW_TILE
    assert d_model == D_MODEL and seq % (tm * TILES_PER_STEP) == 0 and TILES_PER_STEP == 2
    assert a_w_in.shape[0] == 1 and b_w_in.shape[0] == 1
    tiles_per_seq = seq // tm
    n_tiles = bsz * tiles_per_seq
    bf16 = jnp.bfloat16
    f32 = jnp.float32
    smem = pl.BlockSpec(memory_space=pltpu.SMEM)
    hbm = pl.BlockSpec(memory_space=pl.ANY)
    weights = (a_w_in[0], a_w_out[0], w_kv, b_w_in[0], b_w_out[0])
    assert all(w.dtype == f32 and w.shape[0] == D_MODEL for w in weights)
    vecs = jnp.concatenate(
        [g.reshape(1, D_MODEL) for g in (a_pre_norm, a_post_norm, kv_norm, b_pre_norm, b_post_norm)]
        + [a_conv_w[0]], axis=0).astype(f32)
    bucket = _bucket_table()
    x2d = x.reshape(bsz * seq, D_MODEL)

    in_specs = [
        smem,
        smem,
        _resident(bucket.shape),
        _resident(vecs.shape),
        _resident((tm, D_MODEL)),
        pl.BlockSpec((tm, D_MODEL), lambda s: (2 * s + 1, 0)),
        pl.BlockSpec((tm, D_MODEL), lambda s: (jnp.minimum(2 * s + 2, n_tiles - 1), 0)),
    ] + [hbm] * len(weights)
    per_set = 2
    vmem_scratch = [((w.shape[1] // COL_CHUNK, w.shape[0], COL_CHUNK), bf16) for w in weights] + [
        ((STAGE_SLOTS, D_MODEL, WEIGHT_CHUNK), f32),
        ((2, N_Q_HEADS, BLOCK, 2 * BLOCK), f32),
        ((per_set, tm, D_MODEL), bf16),
        ((D_MODEL // COL_CHUNK, tm + SUBLANES, COL_CHUNK), f32),
        ((D_MODEL // COL_CHUNK, tm, COL_CHUNK), bf16),
        ((per_set, tm, D_MODEL), f32),
        ((per_set, N_PAIRS, tm, LANES), bf16),
        ((per_set, N_PAIRS, tm, LANES), f32),
        ((per_set, 2 * N_KV_HEADS, tm + BLOCK, LANES), bf16),
        ((per_set, 2 * N_KV_HEADS, tm + BLOCK, LANES), bf16),
        ((N_PAIRS, tm, LANES), bf16),
        ((SCORE_SLOTS, HEADS_PER_VREG, BLOCK, 2 * BLOCK), f32),
        ((tm, COL_CHUNK), f32),
    ]
    n_weights = len(weights)
    scratch_shapes = ([pltpu.VMEM(*sd) for sd in vmem_scratch[:n_weights + 1]]
                      + [pltpu.SemaphoreType.DMA((STAGE_SLOTS,))]
                      + [pltpu.VMEM(*sd) for sd in vmem_scratch[n_weights + 1:]])
    vmem_limit = (sum(_nbytes(*sd) for sd in vmem_scratch)
                  + PIPELINE_BUFFERS * (2 + TILES_PER_STEP) * _nbytes((tm, D_MODEL), f32)
                  + _nbytes((tm, D_MODEL), f32)
                  + _nbytes(bucket.shape, bucket.dtype) + _nbytes(vecs.shape, f32)
                  + SPILL_ALLOWANCE_BYTES)
    out = pl.pallas_call(
        functools.partial(_trunk_kernel, tiles_per_seq=tiles_per_seq),
        grid=(n_tiles // TILES_PER_STEP,),
        in_specs=in_specs,
        out_specs=pl.BlockSpec((TILES_PER_STEP * tm, D_MODEL), lambda s: (s, 0)),
        out_shape=jax.ShapeDtypeStruct((bsz * seq, D_MODEL), f32),
        scratch_shapes=scratch_shapes,
        compiler_params=pltpu.CompilerParams(
            dimension_semantics=("arbitrary",),
            vmem_limit_bytes=vmem_limit),
        name="yoco_trunk",
    )(rel_bias.astype(f32), b_sinks.reshape(1, N_Q_HEADS).astype(f32),
      jnp.asarray(bucket), vecs, x2d, x2d, x2d, *weights)
    return out.reshape(bsz, seq, D_MODEL)
```

```python
import functools
import math

import jax
import jax.numpy as jnp
import numpy as np
from jax.experimental import pallas as pl
from jax.experimental.pallas import tpu as pltpu

D_MODEL = 1024
CONV_K = 3
HEAD_DIM = 64
N_Q_HEADS = D_MODEL // HEAD_DIM
N_KV_HEADS = 2
GROUP = N_Q_HEADS // N_KV_HEADS
KV_WIDTH = N_KV_HEADS * HEAD_DIM
BLOCK = 128
N_BUCKETS = 32
MAX_DISTANCE = 128
EPS = 1e-6
NEG_INF = -1e30

LANES = 128
SUBLANES = 8
HEADS_PER_VREG = LANES // HEAD_DIM
N_PAIRS = N_Q_HEADS // HEADS_PER_VREG
PAIRS_PER_KV = GROUP // HEADS_PER_VREG
ROW_TILE = 256
TILES_PER_STEP = 2
COL_CHUNK = 256
GATE_SLOT = (2, 0, 1, 3)
SCORE_SLOTS = 3
WEIGHT_CHUNK = 512
STAGE_SLOTS = 3
PIPELINE_BUFFERS = 2
SPILL_ALLOWANCE_BYTES = 12 * 1024 * 1024


def _bucket_table():
    q_loc = np.arange(BLOCK, dtype=np.int32)[:, None]
    s_loc = np.arange(2 * BLOCK, dtype=np.int32)[None, :]
    dist = q_loc + BLOCK - s_loc
    in_window = (dist >= 0) & (dist < BLOCK)
    d = np.maximum(dist, 0)
    max_exact = N_BUCKETS // 2
    df = np.maximum(d, 1).astype(np.float32)
    large = max_exact + (
        np.log(df / np.float32(max_exact)) / np.float32(math.log(MAX_DISTANCE / max_exact))
        * np.float32(N_BUCKETS - max_exact)).astype(np.int32)
    large = np.minimum(large, N_BUCKETS - 1)
    bucket = np.where(d < max_exact, d, large)
    return np.where(in_window, bucket, -1).astype(np.int32)


def _rms_scale(x):
    return x * jax.lax.rsqrt(jnp.mean(x * x, axis=-1, keepdims=True) + EPS)


def _trunk_kernel(rel_bias_ref, sinks_ref, bucket_ref, vecs_ref, x_first_ref, x_odd_ref, x_even_ref,
                  w_in_hbm, w_out_a_hbm, w_kv_hbm, w_in_b_hbm, w_out_b_hbm,
                  out_ref,
                  w_in_ref, w_out_a_ref, w_kv_ref, w_in_b_ref, w_out_b_ref, stage_scr, stage_sem,
                  bias_scr, hn_scr, v_scr, y_scr, h1_scr, q_scr, z_scr, kz_scr, vz_scr, g_scr,
                  s_scr, gate_scr,
                  *, tiles_per_seq):
    step = pl.program_id(0)
    tm = ROW_TILE
    f32 = jnp.float32
    bf16 = jnp.bfloat16
    a_pre_ref, a_post_ref, kv_norm_ref, b_pre_ref, b_post_ref = (
        vecs_ref.at[i:i + 1, :] for i in range(5))
    conv_w_ref = vecs_ref.at[5:5 + CONV_K, :]

    n_blocks = (tm // BLOCK) * N_PAIRS
    n_gate_dots = 2 * (D_MODEL // COL_CHUNK)
    blocks_per_round, rem = divmod(n_blocks, n_gate_dots)
    assert rem == 0 and blocks_per_round >= 1

    def wcols(w_ref, c0, c1):
        return jnp.concatenate([w_ref[t] for t in range(c0 // COL_CHUNK, c1 // COL_CHUNK)], axis=1)

    def block_slices(k):
        j, pair = divmod(k, N_PAIRS)
        return (slice(j * BLOCK, (j + 1) * BLOCK), slice(j * BLOCK, (j + 2) * BLOCK),
                slice(pair * LANES, (pair + 1) * LANES), j, pair)

    def input_norm(x_ref, slot):
        hn_scr[slot] = (_rms_scale(x_ref[...]) * a_pre_ref[...]).astype(bf16)

    def half_step(attn_set, seq_start_b, x_ref, conv_set, seq_start_a, out_rows, mid_hook=None):
        def b_scores(k):
            rows, key_rows, lanes, _, pair = block_slices(k)
            q_pair = q_scr[attn_set, pair, rows, :]
            for parity in range(HEADS_PER_VREG):
                s_scr[k % SCORE_SLOTS, parity] = jax.lax.dot_general(
                    q_pair,
                    kz_scr[attn_set, (pair // PAIRS_PER_KV) * HEADS_PER_VREG + parity, key_rows, :],
                    (((1,), (1,)), ((), ())), preferred_element_type=f32)

        def b_finish(k):
            rows, key_rows, lanes, j, pair = block_slices(k)
            first_block = jnp.where(seq_start_b, 1, 0) if (j == 0 and seq_start_b is not False) else 0
            o_pair = None
            for parity in range(HEADS_PER_VREG):
                head = pair * HEADS_PER_VREG + parity
                slot = (pair // PAIRS_PER_KV) * HEADS_PER_VREG + parity
                s = s_scr[k % SCORE_SLOTS, parity] + bias_scr[first_block, head]
                sink = sinks_ref[0, head]
                m = jnp.maximum(jnp.max(s, axis=-1, keepdims=True), sink)
                p = jnp.exp(s - m)
                denom = jnp.sum(p, axis=-1, keepdims=True) + jnp.exp(sink - m)
                pv = jnp.dot(p.astype(bf16), vz_scr[attn_set, slot, key_rows, :],
                             preferred_element_type=f32)
                pv = pv * (1.0 / denom)
                o_pair = pv if o_pair is None else o_pair + pv
            zg = z_scr[attn_set, pair, rows, :]
            g_scr[pair, rows, :] = (o_pair * (zg * jax.nn.sigmoid(zg))).astype(bf16)

        def a_gate_dot(i):
            chunk, half = divmod(i, 2)
            cols = slice(chunk * COL_CHUNK, (chunk + 1) * COL_CHUNK)
            w0 = (chunk * 4 + 2 * half) * COL_CHUNK
            pair = jnp.dot(hn_scr[conv_set], wcols(w_in_ref, w0, w0 + 2 * COL_CHUNK),
                           preferred_element_type=f32)
            first, second = pair[:, 0:COL_CHUNK], pair[:, COL_CHUNK:2 * COL_CHUNK]
            if half == 0:
                v = first * second
                if seq_start_a is True:
                    prev = jnp.zeros((SUBLANES, COL_CHUNK), f32)
                elif seq_start_a is False:
                    prev = v_scr[chunk]
                else:
                    prev = jnp.where(seq_start_a, 0.0, v_scr[chunk])
                row = jax.lax.broadcasted_iota(jnp.int32, (SUBLANES, COL_CHUNK), 0)

                def delayed(k):
                    rolled = pltpu.roll(v, k, axis=0)
                    head = jnp.where(row < k, pltpu.roll(prev, k, axis=0), rolled[0:SUBLANES])
                    return jnp.concatenate([head, rolled[SUBLANES:]], axis=0)

                gate_scr[...] = (conv_w_ref[0:1, cols] * delayed(2)
                                 + conv_w_ref[1:2, cols] * delayed(1)
                                 + conv_w_ref[2:3, cols] * v)
                v_scr[chunk] = v[tm - SUBLANES:tm]
            else:
                y_scr[chunk] = (first * gate_scr[...]
                                  * (second * jax.nn.sigmoid(second))).astype(bf16)

        if attn_set is not None:
            for k in range(SCORE_SLOTS - 1):
                b_scores(k)
        for r in range(n_gate_dots):
            a_gate_dot(r)
            if attn_set is not None:
                for k in range(r * blocks_per_round, (r + 1) * blocks_per_round):
                    b_finish(k)
                    if k + SCORE_SLOTS - 1 < n_blocks:
                        b_scores(k + SCORE_SLOTS - 1)
            if mid_hook is not None and r == n_gate_dots // 2:
                mid_hook()

        y_in = jnp.concatenate([y_scr[c] for c in range(D_MODEL // COL_CHUNK)], axis=1)
        y = jnp.dot(y_in, wcols(w_out_a_ref, 0, D_MODEL), preferred_element_type=f32)
        if attn_set is not None:
            g_in = jnp.concatenate([g_scr[p] for p in range(N_PAIRS)], axis=1)
            y2 = jnp.dot(g_in, wcols(w_out_b_ref, 0, D_MODEL), preferred_element_type=f32)
        for scr in (kz_scr, vz_scr):
            if seq_start_a is True:
                scr[conv_set, :, 0:BLOCK, :] = jnp.zeros((2 * N_KV_HEADS, BLOCK, LANES), bf16)
            elif seq_start_a is False:
                scr[conv_set, :, 0:BLOCK, :] = scr[1 - conv_set, :, tm:tm + BLOCK, :]
            else:
                scr[conv_set, :, 0:BLOCK, :] = jnp.where(
                    seq_start_a, 0.0, scr[1 - conv_set, :, tm:tm + BLOCK, :].astype(f32)).astype(bf16)
        h1 = x_ref[...] + _rms_scale(y) * a_post_ref[...]
        h1_scr[conv_set] = h1
        h1n = _rms_scale(h1)
        kv = jnp.dot((h1n * kv_norm_ref[...]).astype(bf16), wcols(w_kv_ref, 0, 2 * KV_WIDTH),
                     preferred_element_type=f32)
        low_half = jax.lax.broadcasted_iota(jnp.int32, (tm, LANES), 1) < HEAD_DIM
        for src, dst in ((kv[:, 0:KV_WIDTH], kz_scr), (kv[:, KV_WIDTH:2 * KV_WIDTH], vz_scr)):
            swapped = pltpu.roll(src, HEAD_DIM, axis=1)
            dst[conv_set, 0, BLOCK:BLOCK + tm, :] = jnp.where(low_half, src, 0.0).astype(bf16)
            dst[conv_set, 1, BLOCK:BLOCK + tm, :] = jnp.where(low_half, 0.0, swapped).astype(bf16)
            dst[conv_set, 2, BLOCK:BLOCK + tm, :] = jnp.where(low_half, swapped, 0.0).astype(bf16)
            dst[conv_set, 3, BLOCK:BLOCK + tm, :] = jnp.where(low_half, 0.0, src).astype(bf16)
        qn = (h1n * b_pre_ref[...]).astype(bf16)
        q = jnp.dot(qn, wcols(w_in_b_ref, 0, D_MODEL), preferred_element_type=f32)
        for pair in range(N_PAIRS):
            q_scr[conv_set, pair] = (q[:, pair * LANES:(pair + 1) * LANES]
                                     * (HEAD_DIM ** -0.5)).astype(bf16)
        if attn_set is not None:
            out_ref[out_rows, :] = h1_scr[attn_set] + _rms_scale(y2) * b_post_ref[...]
        z = jnp.dot(qn, wcols(w_in_b_ref, D_MODEL, 2 * D_MODEL), preferred_element_type=f32)
        for pair in range(N_PAIRS):
            z_scr[conv_set, pair] = z[:, pair * LANES:(pair + 1) * LANES]

    @pl.when(step == 0)
    def _init():
        bucket = bucket_ref[...]
        has_prev = jax.lax.broadcasted_iota(jnp.int32, bucket.shape, 1) >= BLOCK

        def build_bias(h):
            acc = jnp.full(bucket.shape, NEG_INF, f32)
            for b in range(N_BUCKETS):
                acc = jnp.where(bucket == b, rel_bias_ref[b, h], acc)
            bias_scr[0, h] = acc
            bias_scr[1, h] = jnp.where(has_prev, acc, NEG_INF)

        chunks = [(src, c0, min(WEIGHT_CHUNK, src.shape[1]), dst)
                  for src, dst in ((w_in_hbm, w_in_ref), (w_out_a_hbm, w_out_a_ref),
                                   (w_kv_hbm, w_kv_ref), (w_in_b_hbm, w_in_b_ref),
                                   (w_out_b_hbm, w_out_b_ref))
                  for c0 in range(0, src.shape[1], WEIGHT_CHUNK)]

        def chunk_copy(i):
            src, c0, width, _ = chunks[i]
            slot = i % STAGE_SLOTS
            return pltpu.make_async_copy(src.at[:, c0:c0 + width],
                                         stage_scr.at[slot, :, 0:width], stage_sem.at[slot])

        def gate_major_col(col):
            gate, rest = divmod(col, D_MODEL)
            chunk, lane = divmod(rest, COL_CHUNK)
            return (chunk * 4 + GATE_SLOT[gate]) * COL_CHUNK + lane

        for i in range(min(STAGE_SLOTS, len(chunks))):
            chunk_copy(i).start()
        for i, (_, c0, width, dst) in enumerate(chunks):
            if i < N_Q_HEADS:
                build_bias(i)
            chunk_copy(i).wait()
            for off in range(0, width, COL_CHUNK):
                d0 = gate_major_col(c0 + off) if dst is w_in_ref else c0 + off
                dst[d0 // COL_CHUNK] = stage_scr[i % STAGE_SLOTS, :, off:off + COL_CHUNK].astype(bf16)
            if i + STAGE_SLOTS < len(chunks):
                chunk_copy(i + STAGE_SLOTS).start()
        for h in range(len(chunks), N_Q_HEADS):
            build_bias(h)
        input_norm(x_first_ref, 0)
        half_step(None, False, x_first_ref, 0, True, None)

    even_seq_start = jax.lax.rem(TILES_PER_STEP * step + TILES_PER_STEP, tiles_per_seq) == 0
    attn_seq_start = jax.lax.rem(TILES_PER_STEP * step, tiles_per_seq) == 0
    input_norm(x_odd_ref, 1)
    half_step(0, attn_seq_start, x_odd_ref, 1, False, slice(0, tm),
              mid_hook=lambda: input_norm(x_even_ref, 0))
    half_step(1, False, x_even_ref, 0, even_seq_start, slice(tm, 2 * tm))


def _resident(shape):
    return pl.BlockSpec(shape, lambda s: (0,) * len(shape), pipeline_mode=pl.Buffered(1))


def _nbytes(shape, dtype):
    return math.prod(shape) * jnp.dtype(dtype).itemsize


@jax.jit
def kernel(x, a_pre_norm, a_w_in, a_conv_w, a_w_out, a_post_norm, kv_norm, w_kv, rel_bias,
           b_pre_norm, b_w_in, b_sinks, b_w_out, b_post_norm):
    bsz, seq, d_model = x.shape
    tm = ROW_TILE
    assert d_model == D_MODEL and seq % (tm * TILES_PER_STEP) == 0 and TILES_PER_STEP == 2
    assert a_w_in.shape[0] == 1 and b_w_in.shape[0] == 1
    tiles_per_seq = seq // tm
    n_tiles = bsz * tiles_per_seq
    bf16 = jnp.bfloat16
    f32 = jnp.float32
    smem = pl.BlockSpec(memory_space=pltpu.SMEM)
    hbm = pl.BlockSpec(memory_space=pl.ANY)
    weights = (a_w_in[0], a_w_out[0], w_kv, b_w_in[0], b_w_out[0])
    assert all(w.dtype == f32 and w.shape[0] == D_MODEL for w in weights)
    vecs = jnp.concatenate(
        [g.reshape(1, D_MODEL) for g in (a_pre_norm, a_post_norm, kv_norm, b_pre_norm, b_post_norm)]
        + [a_conv_w[0]], axis=0).astype(f32)
    bucket = _bucket_table()
    x2d = x.reshape(bsz * seq, D_MODEL)

    in_specs = [
        smem,
        smem,
        _resident(bucket.shape),
        _resident(vecs.shape),
        _resident((tm, D_MODEL)),
        pl.BlockSpec((tm, D_MODEL), lambda s: (2 * s + 1, 0)),
        pl.BlockSpec((tm, D_MODEL), lambda s: (jnp.minimum(2 * s + 2, n_tiles - 1), 0)),
    ] + [hbm] * len(weights)
    per_set = 2
    vmem_scratch = [((w.shape[1] // COL_CHUNK, w.shape[0], COL_CHUNK), bf16) for w in weights] + [
        ((STAGE_SLOTS, D_MODEL, WEIGHT_CHUNK), f32),
        ((2, N_Q_HEADS, BLOCK, 2 * BLOCK), f32),
        ((per_set, tm, D_MODEL), bf16),
        ((D_MODEL // COL_CHUNK, SUBLANES, COL_CHUNK), f32),
        ((D_MODEL // COL_CHUNK, tm, COL_CHUNK), bf16),
        ((per_set, tm, D_MODEL), f32),
        ((per_set, N_PAIRS, tm, LANES), bf16),
        ((per_set, N_PAIRS, tm, LANES), f32),
        ((per_set, 2 * N_KV_HEADS, tm + BLOCK, LANES), bf16),
        ((per_set, 2 * N_KV_HEADS, tm + BLOCK, LANES), bf16),
        ((N_PAIRS, tm, LANES), bf16),
        ((SCORE_SLOTS, HEADS_PER_VREG, BLOCK, 2 * BLOCK), f32),
        ((tm, COL_CHUNK), f32),
    ]
    n_weights = len(weights)
    scratch_shapes = ([pltpu.VMEM(*sd) for sd in vmem_scratch[:n_weights + 1]]
                      + [pltpu.SemaphoreType.DMA((STAGE_SLOTS,))]
                      + [pltpu.VMEM(*sd) for sd in vmem_scratch[n_weights + 1:]])
    vmem_limit = (sum(_nbytes(*sd) for sd in vmem_scratch)
                  + PIPELINE_BUFFERS * (2 + TILES_PER_STEP) * _nbytes((tm, D_MODEL), f32)
                  + _nbytes((tm, D_MODEL), f32)
                  + _nbytes(bucket.shape, bucket.dtype) + _nbytes(vecs.shape, f32)
                  + SPILL_ALLOWANCE_BYTES)
    out = pl.pallas_call(
        functools.partial(_trunk_kernel, tiles_per_seq=tiles_per_seq),
        grid=(n_tiles // TILES_PER_STEP,),
        in_specs=in_specs,
        out_specs=pl.BlockSpec((TILES_PER_STEP * tm, D_MODEL), lambda s: (s, 0)),
        out_shape=jax.ShapeDtypeStruct((bsz * seq, D_MODEL), f32),
        scratch_shapes=scratch_shapes,
        compiler_params=pltpu.CompilerParams(
            dimension_semantics=("arbitrary",),
            vmem_limit_bytes=vmem_limit),
        name="yoco_trunk",
    )(rel_bias.astype(f32), b_sinks.reshape(1, N_Q_HEADS).astype(f32),
      jnp.asarray(bucket), vecs, x2d, x2d, x2d, *weights)
    return out.reshape(bsz, seq, D_MODEL)
```

```python
import functools
import math

import jax
import jax.numpy as jnp
import numpy as np
from jax.experimental import pallas as pl
from jax.experimental.pallas import tpu as pltpu

D_MODEL = 1024
CONV_K = 3
HEAD_DIM = 64
N_Q_HEADS = D_MODEL // HEAD_DIM
N_KV_HEADS = 2
GROUP = N_Q_HEADS // N_KV_HEADS
KV_WIDTH = N_KV_HEADS * HEAD_DIM
BLOCK = 128
N_BUCKETS = 32
MAX_DISTANCE = 128
EPS = 1e-6
NEG_INF = -1e30

LANES = 128
SUBLANES = 8
HEADS_PER_VREG = LANES // HEAD_DIM
N_PAIRS = N_Q_HEADS // HEADS_PER_VREG
PAIRS_PER_KV = GROUP // HEADS_PER_VREG
ROW_TILE = 256
TILES_PER_STEP = 2
COL_CHUNK = 256
GATE_SLOT = (2, 0, 1, 3)
SCORE_SLOTS = 3
WEIGHT_CHUNK = 512
STAGE_SLOTS = 3
PIPELINE_BUFFERS = 2
SPILL_ALLOWANCE_BYTES = 12 * 1024 * 1024


def _bucket_table():
    q_loc = np.arange(BLOCK, dtype=np.int32)[:, None]
    s_loc = np.arange(2 * BLOCK, dtype=np.int32)[None, :]
    dist = q_loc + BLOCK - s_loc
    in_window = (dist >= 0) & (dist < BLOCK)
    d = np.maximum(dist, 0)
    max_exact = N_BUCKETS // 2
    df = np.maximum(d, 1).astype(np.float32)
    large = max_exact + (
        np.log(df / np.float32(max_exact)) / np.float32(math.log(MAX_DISTANCE / max_exact))
        * np.float32(N_BUCKETS - max_exact)).astype(np.int32)
    large = np.minimum(large, N_BUCKETS - 1)
    bucket = np.where(d < max_exact, d, large)
    return np.where(in_window, bucket, -1).astype(np.int32)


def _rms_scale(x):
    return x * jax.lax.rsqrt(jnp.mean(x * x, axis=-1, keepdims=True) + EPS)


def _trunk_kernel(rel_bias_ref, sinks_ref, bucket_ref, vecs_ref, x_first_ref, x_odd_ref, x_even_ref,
                  w_in_hbm, w_out_a_hbm, w_kv_hbm, w_in_b_hbm, w_out_b_hbm,
                  out_ref,
                  w_in_ref, w_out_a_ref, w_kv_ref, w_in_b_ref, w_out_b_ref, stage_scr, stage_sem,
                  bias_scr, hn_scr, v_scr, y_scr, h1_scr, q_scr, z_scr, kz_scr, vz_scr, g_scr,
                  s_scr, gate_scr,
                  *, tiles_per_seq):
    step = pl.program_id(0)
    tm = ROW_TILE
    f32 = jnp.float32
    bf16 = jnp.bfloat16
    a_pre_ref, a_post_ref, kv_norm_ref, b_pre_ref, b_post_ref = (
        vecs_ref.at[i:i + 1, :] for i in range(5))
    conv_w_ref = vecs_ref.at[5:5 + CONV_K, :]

    n_blocks = (tm // BLOCK) * N_PAIRS
    n_gate_dots = 2 * (D_MODEL // COL_CHUNK)
    blocks_per_round, rem = divmod(n_blocks, n_gate_dots)
    assert rem == 0 and blocks_per_round >= 1

    def wcols(w_ref, c0, c1):
        return jnp.concatenate([w_ref[t] for t in range(c0 // COL_CHUNK, c1 // COL_CHUNK)], axis=1)

    def block_slices(k):
        j, pair = divmod(k, N_PAIRS)
        return (slice(j * BLOCK, (j + 1) * BLOCK), slice(j * BLOCK, (j + 2) * BLOCK),
                slice(pair * LANES, (pair + 1) * LANES), j, pair)

    def input_norm(x_ref, slot):
        hn_scr[slot] = (_rms_scale(x_ref[...]) * a_pre_ref[...]).astype(bf16)

    def half_step(attn_set, seq_start_b, x_ref, conv_set, seq_start_a, out_rows, mid_hook=None):
        def b_scores(k):
            rows, key_rows, lanes, _, pair = block_slices(k)
            q_pair = q_scr[attn_set, pair, rows, :]
            for parity in range(HEADS_PER_VREG):
                s_scr[k % SCORE_SLOTS, parity] = jax.lax.dot_general(
                    q_pair,
                    kz_scr[attn_set, (pair // PAIRS_PER_KV) * HEADS_PER_VREG + parity, key_rows, :],
                    (((1,), (1,)), ((), ())), preferred_element_type=f32)

        sink_lane = jax.lax.broadcasted_iota(jnp.int32, (BLOCK, LANES), 1) == 0

        def b_finish(k):
            rows, key_rows, lanes, j, pair = block_slices(k)
            first_block = jnp.where(seq_start_b, 1, 0) if (j == 0 and seq_start_b is not False) else 0
            o_pair = None
            for parity in range(HEADS_PER_VREG):
                head = pair * HEADS_PER_VREG + parity
                slot = (pair // PAIRS_PER_KV) * HEADS_PER_VREG + parity
                s = s_scr[k % SCORE_SLOTS, parity] + bias_scr[first_block, head]
                s = jnp.concatenate(
                    [jnp.where(sink_lane, sinks_ref[0, head], s[:, 0:LANES]), s[:, LANES:]], axis=1)
                m = jnp.max(s, axis=-1, keepdims=True)
                p = jnp.exp(s - m)
                denom = jnp.sum(p, axis=-1, keepdims=True)
                p = p.astype(bf16)
                p = jnp.concatenate(
                    [jnp.where(sink_lane, jnp.zeros((), bf16), p[:, 0:LANES]), p[:, LANES:]], axis=1)
                pv = jnp.dot(p, vz_scr[attn_set, slot, key_rows, :],
                             preferred_element_type=f32)
                pv = pv * (1.0 / denom)
                o_pair = pv if o_pair is None else o_pair + pv
            zg = z_scr[attn_set, pair, rows, :]
            g_scr[pair, rows, :] = (o_pair * (zg * jax.nn.sigmoid(zg))).astype(bf16)

        def a_gate_dot(i):
            chunk, half = divmod(i, 2)
            cols = slice(chunk * COL_CHUNK, (chunk + 1) * COL_CHUNK)
            w0 = (chunk * 4 + 2 * half) * COL_CHUNK
            pair = jnp.dot(hn_scr[conv_set], wcols(w_in_ref, w0, w0 + 2 * COL_CHUNK),
                           preferred_element_type=f32)
            first, second = pair[:, 0:COL_CHUNK], pair[:, COL_CHUNK:2 * COL_CHUNK]
            if half == 0:
                v = first * second
                if seq_start_a is True:
                    prev = jnp.zeros((SUBLANES, COL_CHUNK), f32)
                elif seq_start_a is False:
                    prev = v_scr[chunk]
                else:
                    prev = jnp.where(seq_start_a, 0.0, v_scr[chunk])
                row = jax.lax.broadcasted_iota(jnp.int32, (SUBLANES, COL_CHUNK), 0)

                def delayed(k):
                    rolled = pltpu.roll(v, k, axis=0)
                    head = jnp.where(row < k, pltpu.roll(prev, k, axis=0), rolled[0:SUBLANES])
                    return jnp.concatenate([head, rolled[SUBLANES:]], axis=0)

                gate_scr[...] = (conv_w_ref[0:1, cols] * delayed(2)
                                 + conv_w_ref[1:2, cols] * delayed(1)
                                 + conv_w_ref[2:3, cols] * v)
                v_scr[chunk] = v[tm - SUBLANES:tm]
            else:
                y_scr[chunk] = (first * gate_scr[...]
                                  * (second * jax.nn.sigmoid(second))).astype(bf16)

        if attn_set is not None:
            for k in range(SCORE_SLOTS - 1):
                b_scores(k)
        for r in range(n_gate_dots):
            a_gate_dot(r)
            if attn_set is not None:
                for k in range(r * blocks_per_round, (r + 1) * blocks_per_round):
                    b_finish(k)
                    if k + SCORE_SLOTS - 1 < n_blocks:
                        b_scores(k + SCORE_SLOTS - 1)
            if mid_hook is not None and r == n_gate_dots // 2:
                mid_hook()

        y_in = jnp.concatenate([y_scr[c] for c in range(D_MODEL // COL_CHUNK)], axis=1)
        y = jnp.dot(y_in, wcols(w_out_a_ref, 0, D_MODEL), preferred_element_type=f32)
        if attn_set is not None:
            g_in = jnp.concatenate([g_scr[p] for p in range(N_PAIRS)], axis=1)
            y2 = jnp.dot(g_in, wcols(w_out_b_ref, 0, D_MODEL), preferred_element_type=f32)
        for scr in (kz_scr, vz_scr):
            if seq_start_a is True:
                scr[conv_set, :, 0:BLOCK, :] = jnp.zeros((2 * N_KV_HEADS, BLOCK, LANES), bf16)
            elif seq_start_a is False:
                scr[conv_set, :, 0:BLOCK, :] = scr[1 - conv_set, :, tm:tm + BLOCK, :]
            else:
                scr[conv_set, :, 0:BLOCK, :] = jnp.where(
                    seq_start_a, 0.0, scr[1 - conv_set, :, tm:tm + BLOCK, :].astype(f32)).astype(bf16)
        h1 = x_ref[...] + _rms_scale(y) * a_post_ref[...]
        h1_scr[conv_set] = h1
        h1n = _rms_scale(h1)
        kv = jnp.dot((h1n * kv_norm_ref[...]).astype(bf16), wcols(w_kv_ref, 0, 2 * KV_WIDTH),
                     preferred_element_type=f32)
        low_half = jax.lax.broadcasted_iota(jnp.int32, (tm, LANES), 1) < HEAD_DIM
        for src, dst in ((kv[:, 0:KV_WIDTH], kz_scr), (kv[:, KV_WIDTH:2 * KV_WIDTH], vz_scr)):
            swapped = pltpu.roll(src, HEAD_DIM, axis=1)
            dst[conv_set, 0, BLOCK:BLOCK + tm, :] = jnp.where(low_half, src, 0.0).astype(bf16)
            dst[conv_set, 1, BLOCK:BLOCK + tm, :] = jnp.where(low_half, 0.0, swapped).astype(bf16)
            dst[conv_set, 2, BLOCK:BLOCK + tm, :] = jnp.where(low_half, swapped, 0.0).astype(bf16)
            dst[conv_set, 3, BLOCK:BLOCK + tm, :] = jnp.where(low_half, 0.0, src).astype(bf16)
        qn = (h1n * b_pre_ref[...]).astype(bf16)
        q = jnp.dot(qn, wcols(w_in_b_ref, 0, D_MODEL), preferred_element_type=f32)
        for pair in range(N_PAIRS):
            q_scr[conv_set, pair] = (q[:, pair * LANES:(pair + 1) * LANES]
                                     * (HEAD_DIM ** -0.5)).astype(bf16)
        if attn_set is not None:
            out_ref[out_rows, :] = h1_scr[attn_set] + _rms_scale(y2) * b_post_ref[...]
        z = jnp.dot(qn, wcols(w_in_b_ref, D_MODEL, 2 * D_MODEL), preferred_element_type=f32)
        for pair in range(N_PAIRS):
            z_scr[conv_set, pair] = z[:, pair * LANES:(pair + 1) * LANES]

    @pl.when(step == 0)
    def _init():
        bucket = bucket_ref[...]
        has_prev = jax.lax.broadcasted_iota(jnp.int32, bucket.shape, 1) >= BLOCK

        def build_bias(h):
            acc = jnp.full(bucket.shape, NEG_INF, f32)
            for b in range(N_BUCKETS):
                acc = jnp.where(bucket == b, rel_bias_ref[b, h], acc)
            bias_scr[0, h] = acc
            bias_scr[1, h] = jnp.where(has_prev, acc, NEG_INF)

        chunks = [(src, c0, min(WEIGHT_CHUNK, src.shape[1]), dst)
                  for src, dst in ((w_in_hbm, w_in_ref), (w_out_a_hbm, w_out_a_ref),
                                   (w_kv_hbm, w_kv_ref), (w_in_b_hbm, w_in_b_ref),
                                   (w_out_b_hbm, w_out_b_ref))
                  for c0 in range(0, src.shape[1], WEIGHT_CHUNK)]

        def chunk_copy(i):
            src, c0, width, _ = chunks[i]
            slot = i % STAGE_SLOTS
            return pltpu.make_async_copy(src.at[:, c0:c0 + width],
                                         stage_scr.at[slot, :, 0:width], stage_sem.at[slot])

        def gate_major_col(col):
            gate, rest = divmod(col, D_MODEL)
            chunk, lane = divmod(rest, COL_CHUNK)
            return (chunk * 4 + GATE_SLOT[gate]) * COL_CHUNK + lane

        for i in range(min(STAGE_SLOTS, len(chunks))):
            chunk_copy(i).start()
        for i, (_, c0, width, dst) in enumerate(chunks):
            if i < N_Q_HEADS:
                build_bias(i)
            chunk_copy(i).wait()
            for off in range(0, width, COL_CHUNK):
                d0 = gate_major_col(c0 + off) if dst is w_in_ref else c0 + off
                dst[d0 // COL_CHUNK] = stage_scr[i % STAGE_SLOTS, :, off:off + COL_CHUNK].astype(bf16)
            if i + STAGE_SLOTS < len(chunks):
                chunk_copy(i + STAGE_SLOTS).start()
        for h in range(len(chunks), N_Q_HEADS):
            build_bias(h)
        input_norm(x_first_ref, 0)
        half_step(None, False, x_first_ref, 0, True, None)

    even_seq_start = jax.lax.rem(TILES_PER_STEP * step + TILES_PER_STEP, tiles_per_seq) == 0
    attn_seq_start = jax.lax.rem(TILES_PER_STEP * step, tiles_per_seq) == 0
    input_norm(x_odd_ref, 1)
    half_step(0, attn_seq_start, x_odd_ref, 1, False, slice(0, tm),
              mid_hook=lambda: input_norm(x_even_ref, 0))
    half_step(1, False, x_even_ref, 0, even_seq_start, slice(tm, 2 * tm))


def _resident(shape):
    return pl.BlockSpec(shape, lambda s: (0,) * len(shape), pipeline_mode=pl.Buffered(1))


def _nbytes(shape, dtype):
    return math.prod(shape) * jnp.dtype(dtype).itemsize


@jax.jit
def kernel(x, a_pre_norm, a_w_in, a_conv_w, a_w_out, a_post_norm, kv_norm, w_kv, rel_bias,
           b_pre_norm, b_w_in, b_sinks, b_w_out, b_post_norm):
    bsz, seq, d_model = x.shape
    tm = ROW_TILE
    assert d_model == D_MODEL and seq % (tm * TILES_PER_STEP) == 0 and TILES_PER_STEP == 2
    assert a_w_in.shape[0] == 1 and b_w_in.shape[0] == 1
    tiles_per_seq = seq // tm
    n_tiles = bsz * tiles_per_seq
    bf16 = jnp.bfloat16
    f32 = jnp.float32
    smem = pl.BlockSpec(memory_space=pltpu.SMEM)
    hbm = pl.BlockSpec(memory_space=pl.ANY)
    weights = (a_w_in[0], a_w_out[0], w_kv, b_w_in[0], b_w_out[0])
    assert all(w.dtype == f32 and w.shape[0] == D_MODEL for w in weights)
    vecs = jnp.concatenate(
        [g.reshape(1, D_MODEL) for g in (a_pre_norm, a_post_norm, kv_norm, b_pre_norm, b_post_norm)]
        + [a_conv_w[0]], axis=0).astype(f32)
    bucket = _bucket_table()
    x2d = x.reshape(bsz * seq, D_MODEL)

    in_specs = [
        smem,
        smem,
        _resident(bucket.shape),
        _resident(vecs.shape),
        _resident((tm, D_MODEL)),
        pl.BlockSpec((tm, D_MODEL), lambda s: (2 * s + 1, 0)),
        pl.BlockSpec((tm, D_MODEL), lambda s: (jnp.minimum(2 * s + 2, n_tiles - 1), 0)),
    ] + [hbm] * len(weights)
    per_set = 2
    vmem_scratch = [((w.shape[1] // COL_CHUNK, w.shape[0], COL_CHUNK), bf16) for w in weights] + [
        ((STAGE_SLOTS, D_MODEL, WEIGHT_CHUNK), f32),
        ((2, N_Q_HEADS, BLOCK, 2 * BLOCK), f32),
        ((per_set, tm, D_MODEL), bf16),
        ((D_MODEL // COL_CHUNK, SUBLANES, COL_CHUNK), f32),
        ((D_MODEL // COL_CHUNK, tm, COL_CHUNK), bf16),
        ((per_set, tm, D_MODEL), f32),
        ((per_set, N_PAIRS, tm, LANES), bf16),
        ((per_set, N_PAIRS, tm, LANES), f32),
        ((per_set, 2 * N_KV_HEADS, tm + BLOCK, LANES), bf16),
        ((per_set, 2 * N_KV_HEADS, tm + BLOCK, LANES), bf16),
        ((N_PAIRS, tm, LANES), bf16),
        ((SCORE_SLOTS, HEADS_PER_VREG, BLOCK, 2 * BLOCK), f32),
        ((tm, COL_CHUNK), f32),
    ]
    n_weights = len(weights)
    scratch_shapes = ([pltpu.VMEM(*sd) for sd in vmem_scratch[:n_weights + 1]]
                      + [pltpu.SemaphoreType.DMA((STAGE_SLOTS,))]
                      + [pltpu.VMEM(*sd) for sd in vmem_scratch[n_weights + 1:]])
    vmem_limit = (sum(_nbytes(*sd) for sd in vmem_scratch)
                  + PIPELINE_BUFFERS * (2 + TILES_PER_STEP) * _nbytes((tm, D_MODEL), f32)
                  + _nbytes((tm, D_MODEL), f32)
                  + _nbytes(bucket.shape, bucket.dtype) + _nbytes(vecs.shape, f32)
                  + SPILL_ALLOWANCE_BYTES)
    out = pl.pallas_call(
        functools.partial(_trunk_kernel, tiles_per_seq=tiles_per_seq),
        grid=(n_tiles // TILES_PER_STEP,),
        in_specs=in_specs,
        out_specs=pl.BlockSpec((TILES_PER_STEP * tm, D_MODEL), lambda s: (s, 0)),
        out_shape=jax.ShapeDtypeStruct((bsz * seq, D_MODEL), f32),
        scratch_shapes=scratch_shapes,
        compiler_params=pltpu.CompilerParams(
            dimension_semantics=("arbitrary",),
            vmem_limit_bytes=vmem_limit),
        name="yoco_trunk",
    )(rel_bias.astype(f32), b_sinks.reshape(1, N_Q_HEADS).astype(f32),
      jnp.asarray(bucket), vecs, x2d, x2d, x2d, *weights)
    return out.reshape(bsz, seq, D_MODEL)
```

```python
import functools
import math

import jax
import jax.numpy as jnp
import numpy as np
from jax.experimental import pallas as pl
from jax.experimental.pallas import tpu as pltpu

D_MODEL = 1024
CONV_K = 3
HEAD_DIM = 64
N_Q_HEADS = D_MODEL // HEAD_DIM
N_KV_HEADS = 2
GROUP = N_Q_HEADS // N_KV_HEADS
KV_WIDTH = N_KV_HEADS * HEAD_DIM
BLOCK = 128
N_BUCKETS = 32
MAX_DISTANCE = 128
EPS = 1e-6
NEG_INF = -1e30
LOG2_E = math.log2(math.e)

LANES = 128
SUBLANES = 8
HEADS_PER_VREG = LANES // HEAD_DIM
N_PAIRS = N_Q_HEADS // HEADS_PER_VREG
PAIRS_PER_KV = GROUP // HEADS_PER_VREG
ROW_TILE = 256
TILES_PER_STEP = 2
COL_CHUNK = 256
GATE_SLOT = (2, 0, 1, 3)
SCORE_SLOTS = 3
WEIGHT_CHUNK = 512
STAGE_SLOTS = 3
PIPELINE_BUFFERS = 2
SPILL_ALLOWANCE_BYTES = 12 * 1024 * 1024


def _bucket_table():
    q_loc = np.arange(BLOCK, dtype=np.int32)[:, None]
    s_loc = np.arange(2 * BLOCK, dtype=np.int32)[None, :]
    dist = q_loc + BLOCK - s_loc
    in_window = (dist >= 0) & (dist < BLOCK)
    d = np.maximum(dist, 0)
    max_exact = N_BUCKETS // 2
    df = np.maximum(d, 1).astype(np.float32)
    large = max_exact + (
        np.log(df / np.float32(max_exact)) / np.float32(math.log(MAX_DISTANCE / max_exact))
        * np.float32(N_BUCKETS - max_exact)).astype(np.int32)
    large = np.minimum(large, N_BUCKETS - 1)
    bucket = np.where(d < max_exact, d, large)
    return np.where(in_window, bucket, -1).astype(np.int32)


def _rms_scale(x):
    return x * jax.lax.rsqrt(jnp.mean(x * x, axis=-1, keepdims=True) + EPS)


def _trunk_kernel(rel_bias_ref, sinks_ref, bucket_ref, a_pre_ref, a_post_ref, kv_norm_ref, b_pre_ref,
                  b_post_ref, conv_w_ref, x_first_ref, x_odd_ref, x_even_ref,
                  w_in_hbm, w_out_a_hbm, w_kv_hbm, w_in_b_hbm, w_out_b_hbm,
                  out_ref,
                  w_in_ref, w_out_a_ref, w_kv_ref, w_in_b_ref, w_out_b_ref, stage_scr, stage_sem,
                  bias_scr, hn_scr, v_scr, y_scr, h1_scr, q_scr, z_scr, kz_scr, vz_scr, g_scr,
                  s_scr, gate_scr,
                  *, tiles_per_seq):
    step = pl.program_id(0)
    tm = ROW_TILE
    f32 = jnp.float32
    bf16 = jnp.bfloat16

    n_blocks = (tm // BLOCK) * N_PAIRS
    n_gate_dots = 2 * (D_MODEL // COL_CHUNK)
    blocks_per_round, rem = divmod(n_blocks, n_gate_dots)
    assert rem == 0 and blocks_per_round >= 1

    def wcols(w_ref, c0, c1):
        return jnp.concatenate([w_ref[t] for t in range(c0 // COL_CHUNK, c1 // COL_CHUNK)], axis=1)

    def block_slices(k):
        j, pair = divmod(k, N_PAIRS)
        return (slice(j * BLOCK, (j + 1) * BLOCK), slice(j * BLOCK, (j + 2) * BLOCK),
                slice(pair * LANES, (pair + 1) * LANES), j, pair)

    def input_norm(x_ref, slot):
        hn_scr[slot] = (_rms_scale(x_ref[...]) * a_pre_ref[...]).astype(bf16)

    def half_step(attn_set, seq_start_b, x_ref, conv_set, seq_start_a, out_rows, mid_hook=None):
        def b_scores(k):
            rows, key_rows, lanes, _, pair = block_slices(k)
            q_pair = q_scr[attn_set, pair, rows, :]
            for parity in range(HEADS_PER_VREG):
                s_scr[k % SCORE_SLOTS, parity] = jax.lax.dot_general(
                    q_pair,
                    kz_scr[attn_set, (pair // PAIRS_PER_KV) * HEADS_PER_VREG + parity, key_rows, :],
                    (((1,), (1,)), ((), ())), preferred_element_type=f32)

        sink_lane = jax.lax.broadcasted_iota(jnp.int32, (BLOCK, LANES), 1) == 0
        low_lanes = jax.lax.broadcasted_iota(jnp.int32, (BLOCK, LANES), 1) < HEAD_DIM

        def b_finish(k):
            rows, key_rows, lanes, j, pair = block_slices(k)
            first_block = jnp.where(seq_start_b, 1, 0) if (j == 0 and seq_start_b is not False) else 0
            slot0 = (pair // PAIRS_PER_KV) * HEADS_PER_VREG
            probs, inv_denoms = [], []
            for parity in range(HEADS_PER_VREG):
                head = pair * HEADS_PER_VREG + parity
                s = s_scr[k % SCORE_SLOTS, parity] + bias_scr[first_block, head]
                s = jnp.concatenate(
                    [jnp.where(sink_lane, sinks_ref[0, head] * LOG2_E, s[:, 0:LANES]), s[:, LANES:]], axis=1)
                m = jnp.max(s, axis=-1, keepdims=True)
                p = jnp.exp2(s - m)
                inv_denoms.append(1.0 / jnp.sum(p, axis=-1, keepdims=True))
                p = p.astype(bf16)
                probs.append(jnp.concatenate(
                    [jnp.where(sink_lane, jnp.zeros((), bf16), p[:, 0:LANES]), p[:, LANES:]], axis=1))
            values = jnp.concatenate(
                [vz_scr[attn_set, slot0 + parity, key_rows, :] for parity in range(HEADS_PER_VREG)],
                axis=0)
            pv = jnp.dot(jnp.concatenate(probs, axis=1), values, preferred_element_type=f32)
            o_pair = pv * jnp.where(low_lanes, inv_denoms[0], inv_denoms[1])
            zg = z_scr[attn_set, pair, rows, :]
            g_scr[pair, rows, :] = (o_pair * (zg * jax.nn.sigmoid(zg))).astype(bf16)

        def a_gate_dot(i):
            chunk, half = divmod(i, 2)
            cols = slice(chunk * COL_CHUNK, (chunk + 1) * COL_CHUNK)
            w0 = (chunk * 4 + 2 * half) * COL_CHUNK
            pair = jnp.dot(hn_scr[conv_set], wcols(w_in_ref, w0, w0 + 2 * COL_CHUNK),
                           preferred_element_type=f32)
            first, second = pair[:, 0:COL_CHUNK], pair[:, COL_CHUNK:2 * COL_CHUNK]
            if half == 0:
                v = first * second
                if seq_start_a is True:
                    prev = jnp.zeros((SUBLANES, COL_CHUNK), f32)
                elif seq_start_a is False:
                    prev = v_scr[chunk]
                else:
                    prev = jnp.where(seq_start_a, 0.0, v_scr[chunk])
                row = jax.lax.broadcasted_iota(jnp.int32, (SUBLANES, COL_CHUNK), 0)

                def delayed(k):
                    rolled = pltpu.roll(v, k, axis=0)
                    head = jnp.where(row < k, pltpu.roll(prev, k, axis=0), rolled[0:SUBLANES])
                    return jnp.concatenate([head, rolled[SUBLANES:]], axis=0)

                gate_scr[...] = (conv_w_ref[0, :, cols] * delayed(2)
                                 + conv_w_ref[1, :, cols] * delayed(1)
                                 + conv_w_ref[2, :, cols] * v)
                v_scr[chunk] = v[tm - SUBLANES:tm]
            else:
                y_scr[chunk] = (first * gate_scr[...]
                                  * (second * jax.nn.sigmoid(second))).astype(bf16)

        if attn_set is not None:
            for k in range(SCORE_SLOTS - 1):
                b_scores(k)
        for r in range(n_gate_dots):
            a_gate_dot(r)
            if attn_set is not None:
                for k in range(r * blocks_per_round, (r + 1) * blocks_per_round):
                    b_finish(k)
                    if k + SCORE_SLOTS - 1 < n_blocks:
                        b_scores(k + SCORE_SLOTS - 1)
            if mid_hook is not None and r == n_gate_dots // 2:
                mid_hook()

        y_in = jnp.concatenate([y_scr[c] for c in range(D_MODEL // COL_CHUNK)], axis=1)
        y = jnp.dot(y_in, wcols(w_out_a_ref, 0, D_MODEL), preferred_element_type=f32)
        if attn_set is not None:
            g_in = jnp.concatenate([g_scr[p] for p in range(N_PAIRS)], axis=1)
            y2 = jnp.dot(g_in, wcols(w_out_b_ref, 0, D_MODEL), preferred_element_type=f32)
        for scr in (kz_scr, vz_scr):
            if seq_start_a is True:
                scr[conv_set, :, 0:BLOCK, :] = jnp.zeros((2 * N_KV_HEADS, BLOCK, LANES), bf16)
            elif seq_start_a is False:
                scr[conv_set, :, 0:BLOCK, :] = scr[1 - conv_set, :, tm:tm + BLOCK, :]
            else:
                scr[conv_set, :, 0:BLOCK, :] = jnp.where(
                    seq_start_a, 0.0, scr[1 - conv_set, :, tm:tm + BLOCK, :].astype(f32)).astype(bf16)
        h1 = x_ref[...] + _rms_scale(y) * a_post_ref[...]
        h1_scr[conv_set] = h1
        h1n = _rms_scale(h1)
        kv = jnp.dot((h1n * kv_norm_ref[...]).astype(bf16), wcols(w_kv_ref, 0, 2 * KV_WIDTH),
                     preferred_element_type=f32)
        low_half = jax.lax.broadcasted_iota(jnp.int32, (tm, LANES), 1) < HEAD_DIM
        for src, dst in ((kv[:, 0:KV_WIDTH], kz_scr), (kv[:, KV_WIDTH:2 * KV_WIDTH], vz_scr)):
            swapped = pltpu.roll(src, HEAD_DIM, axis=1)
            dst[conv_set, 0, BLOCK:BLOCK + tm, :] = jnp.where(low_half, src, 0.0).astype(bf16)
            dst[conv_set, 1, BLOCK:BLOCK + tm, :] = jnp.where(low_half, 0.0, swapped).astype(bf16)
            dst[conv_set, 2, BLOCK:BLOCK + tm, :] = jnp.where(low_half, swapped, 0.0).astype(bf16)
            dst[conv_set, 3, BLOCK:BLOCK + tm, :] = jnp.where(low_half, 0.0, src).astype(bf16)
        qn = (h1n * b_pre_ref[...]).astype(bf16)
        q = jnp.dot(qn, wcols(w_in_b_ref, 0, D_MODEL), preferred_element_type=f32)
        for pair in range(N_PAIRS):
            q_scr[conv_set, pair] = (q[:, pair * LANES:(pair + 1) * LANES]
                                     * (HEAD_DIM ** -0.5 * LOG2_E)).astype(bf16)
        if attn_set is not None:
            out_ref[out_rows, :] = h1_scr[attn_set] + _rms_scale(y2) * b_post_ref[...]
        z = jnp.dot(qn, wcols(w_in_b_ref, D_MODEL, 2 * D_MODEL), preferred_element_type=f32)
        for pair in range(N_PAIRS):
            z_scr[conv_set, pair] = z[:, pair * LANES:(pair + 1) * LANES]

    @pl.when(step == 0)
    def _init():
        bucket = bucket_ref[...]
        has_prev = jax.lax.broadcasted_iota(jnp.int32, bucket.shape, 1) >= BLOCK

        def build_bias(h):
            acc = jnp.full(bucket.shape, NEG_INF, f32)
            for b in range(N_BUCKETS):
                acc = jnp.where(bucket == b, rel_bias_ref[h, b] * LOG2_E, acc)
            bias_scr[0, h] = acc
            bias_scr[1, h] = jnp.where(has_prev, acc, NEG_INF)

        chunks = [(src, c0, min(WEIGHT_CHUNK, src.shape[1]), dst)
                  for src, dst in ((w_in_hbm, w_in_ref), (w_out_a_hbm, w_out_a_ref),
                                   (w_kv_hbm, w_kv_ref), (w_in_b_hbm, w_in_b_ref),
                                   (w_out_b_hbm, w_out_b_ref))
                  for c0 in range(0, src.shape[1], WEIGHT_CHUNK)]

        def chunk_copy(i):
            src, c0, width, _ = chunks[i]
            slot = i % STAGE_SLOTS
            return pltpu.make_async_copy(src.at[:, c0:c0 + width],
                                         stage_scr.at[slot, :, 0:width], stage_sem.at[slot])

        def gate_major_col(col):
            gate, rest = divmod(col, D_MODEL)
            chunk, lane = divmod(rest, COL_CHUNK)
            return (chunk * 4 + GATE_SLOT[gate]) * COL_CHUNK + lane

        for i in range(min(STAGE_SLOTS, len(chunks))):
            chunk_copy(i).start()
        for i, (_, c0, width, dst) in enumerate(chunks):
            if i < N_Q_HEADS:
                build_bias(i)
            chunk_copy(i).wait()
            for off in range(0, width, COL_CHUNK):
                d0 = gate_major_col(c0 + off) if dst is w_in_ref else c0 + off
                dst[d0 // COL_CHUNK] = stage_scr[i % STAGE_SLOTS, :, off:off + COL_CHUNK].astype(bf16)
            if i + STAGE_SLOTS < len(chunks):
                chunk_copy(i + STAGE_SLOTS).start()
        for h in range(len(chunks), N_Q_HEADS):
            build_bias(h)
        input_norm(x_first_ref, 0)
        half_step(None, False, x_first_ref, 0, True, None)

    even_seq_start = jax.lax.rem(TILES_PER_STEP * step + TILES_PER_STEP, tiles_per_seq) == 0
    attn_seq_start = jax.lax.rem(TILES_PER_STEP * step, tiles_per_seq) == 0
    input_norm(x_odd_ref, 1)
    half_step(0, attn_seq_start, x_odd_ref, 1, False, slice(0, tm),
              mid_hook=lambda: input_norm(x_even_ref, 0))
    half_step(1, False, x_even_ref, 0, even_seq_start, slice(tm, 2 * tm))


def _resident(shape):
    return pl.BlockSpec(shape, lambda s: (0,) * len(shape), pipeline_mode=pl.Buffered(1))


def _nbytes(shape, dtype):
    return math.prod(shape) * jnp.dtype(dtype).itemsize


@jax.jit
def kernel(x, a_pre_norm, a_w_in, a_conv_w, a_w_out, a_post_norm, kv_norm, w_kv, rel_bias,
           b_pre_norm, b_w_in, b_sinks, b_w_out, b_post_norm):
    bsz, seq, d_model = x.shape
    tm = ROW_TILE
    assert d_model == D_MODEL and seq % (tm * TILES_PER_STEP) == 0 and TILES_PER_STEP == 2
    assert a_w_in.shape[0] == 1 and b_w_in.shape[0] == 1
    tiles_per_seq = seq // tm
    n_tiles = bsz * tiles_per_seq
    bf16 = jnp.bfloat16
    f32 = jnp.float32
    smem = pl.BlockSpec(memory_space=pltpu.SMEM)
    hbm = pl.BlockSpec(memory_space=pl.ANY)
    weights = (a_w_in[0], a_w_out[0], w_kv, b_w_in[0], b_w_out[0])
    assert all(w.dtype == f32 and w.shape[0] == D_MODEL for w in weights)
    gains = [g.reshape(1, D_MODEL).astype(f32)
             for g in (a_pre_norm, a_post_norm, kv_norm, b_pre_norm, b_post_norm)]
    assert a_conv_w.shape == (1, CONV_K, D_MODEL)
    conv_taps = a_conv_w.transpose(1, 0, 2).astype(f32)
    bucket = _bucket_table()
    x2d = x.reshape(bsz * seq, D_MODEL)

    in_specs = [
        smem,
        smem,
        _resident(bucket.shape),
    ] + [_resident(g.shape) for g in gains] + [
        _resident(conv_taps.shape),
        _resident((tm, D_MODEL)),
        pl.BlockSpec((tm, D_MODEL), lambda s: (2 * s + 1, 0)),
        pl.BlockSpec((tm, D_MODEL), lambda s: (jnp.minimum(2 * s + 2, n_tiles - 1), 0)),
    ] + [hbm] * len(weights)
    per_set = 2
    vmem_scratch = [((w.shape[1] // COL_CHUNK, w.shape[0], COL_CHUNK), bf16) for w in weights] + [
        ((STAGE_SLOTS, D_MODEL, WEIGHT_CHUNK), f32),
        ((2, N_Q_HEADS, BLOCK, 2 * BLOCK), f32),
        ((per_set, tm, D_MODEL), bf16),
        ((D_MODEL // COL_CHUNK, SUBLANES, COL_CHUNK), f32),
        ((D_MODEL // COL_CHUNK, tm, COL_CHUNK), bf16),
        ((per_set, tm, D_MODEL), f32),
        ((per_set, N_PAIRS, tm, LANES), bf16),
        ((per_set, N_PAIRS, tm, LANES), f32),
        ((per_set, 2 * N_KV_HEADS, tm + BLOCK, LANES), bf16),
        ((per_set, 2 * N_KV_HEADS, tm + BLOCK, LANES), bf16),
        ((N_PAIRS, tm, LANES), bf16),
        ((SCORE_SLOTS, HEADS_PER_VREG, BLOCK, 2 * BLOCK), f32),
        ((tm, COL_CHUNK), f32),
    ]
    n_weights = len(weights)
    scratch_shapes = ([pltpu.VMEM(*sd) for sd in vmem_scratch[:n_weights + 1]]
                      + [pltpu.SemaphoreType.DMA((STAGE_SLOTS,))]
                      + [pltpu.VMEM(*sd) for sd in vmem_scratch[n_weights + 1:]])
    vmem_limit = (sum(_nbytes(*sd) for sd in vmem_scratch)
                  + PIPELINE_BUFFERS * (2 + TILES_PER_STEP) * _nbytes((tm, D_MODEL), f32)
                  + _nbytes((tm, D_MODEL), f32)
                  + _nbytes(bucket.shape, bucket.dtype)
                  + (len(gains) + CONV_K) * _nbytes((SUBLANES, D_MODEL), f32)
                  + SPILL_ALLOWANCE_BYTES)
    out = pl.pallas_call(
        functools.partial(_trunk_kernel, tiles_per_seq=tiles_per_seq),
        grid=(n_tiles // TILES_PER_STEP,),
        in_specs=in_specs,
        out_specs=pl.BlockSpec((TILES_PER_STEP * tm, D_MODEL), lambda s: (s, 0)),
        out_shape=jax.ShapeDtypeStruct((bsz * seq, D_MODEL), f32),
        scratch_shapes=scratch_shapes,
        compiler_params=pltpu.CompilerParams(
            dimension_semantics=("arbitrary",),
            vmem_limit_bytes=vmem_limit),
        name="yoco_trunk",
    )(rel_bias.T.astype(f32), b_sinks.reshape(1, N_Q_HEADS).astype(f32),
      jnp.asarray(bucket), *gains, conv_taps, x2d, x2d, x2d, *weights)
    return out.reshape(bsz, seq, D_MODEL)
```

```python
import functools
import math

import jax
import jax.numpy as jnp
import numpy as np
from jax.experimental import pallas as pl
from jax.experimental.pallas import tpu as pltpu

D_MODEL = 1024
CONV_K = 3
HEAD_DIM = 64
N_Q_HEADS = D_MODEL // HEAD_DIM
N_KV_HEADS = 2
GROUP = N_Q_HEADS // N_KV_HEADS
KV_WIDTH = N_KV_HEADS * HEAD_DIM
BLOCK = 128
N_BUCKETS = 32
MAX_DISTANCE = 128
EPS = 1e-6
NEG_INF = -1e30
LOG2_E = math.log2(math.e)

LANES = 128
SUBLANES = 8
HEADS_PER_VREG = LANES // HEAD_DIM
N_PAIRS = N_Q_HEADS // HEADS_PER_VREG
PAIRS_PER_KV = GROUP // HEADS_PER_VREG
ROW_TILE = 256
TILES_PER_STEP = 2
COL_CHUNK = 256
GATE_SLOT = (2, 0, 1, 3)
SCORE_SLOTS = 3
WEIGHT_CHUNK = 512
STAGE_SLOTS = 3
PIPELINE_BUFFERS = 2
SPILL_ALLOWANCE_BYTES = 12 * 1024 * 1024


def _bucket_table():
    q_loc = np.arange(BLOCK, dtype=np.int32)[:, None]
    s_loc = np.arange(2 * BLOCK, dtype=np.int32)[None, :]
    dist = q_loc + BLOCK - s_loc
    in_window = (dist >= 0) & (dist < BLOCK)
    d = np.maximum(dist, 0)
    max_exact = N_BUCKETS // 2
    df = np.maximum(d, 1).astype(np.float32)
    large = max_exact + (
        np.log(df / np.float32(max_exact)) / np.float32(math.log(MAX_DISTANCE / max_exact))
        * np.float32(N_BUCKETS - max_exact)).astype(np.int32)
    large = np.minimum(large, N_BUCKETS - 1)
    bucket = np.where(d < max_exact, d, large)
    return np.where(in_window, bucket, -1).astype(np.int32)


def _rms_scale(x):
    return x * jax.lax.rsqrt(jnp.mean(x * x, axis=-1, keepdims=True) + EPS)


def _trunk_kernel(rel_bias_ref, sinks_ref, bucket_ref, a_pre_ref, a_post_ref, kv_norm_ref, b_pre_ref,
                  b_post_ref, conv_w_ref, x_first_ref, x_odd_ref, x_even_ref,
                  w_in_hbm, w_out_a_hbm, w_kv_hbm, w_in_b_hbm, w_out_b_hbm,
                  out_ref,
                  w_in_ref, w_out_a_ref, w_kv_ref, w_in_b_ref, w_out_b_ref, stage_scr, stage_sem,
                  bias_scr, hn_scr, v_scr, y_scr, h1_scr, q_scr, z_scr, kz_scr, vz_scr, g_scr,
                  s_scr, gate_scr,
                  *, tiles_per_seq):
    step = pl.program_id(0)
    tm = ROW_TILE
    f32 = jnp.float32
    bf16 = jnp.bfloat16

    n_blocks = (tm // BLOCK) * N_PAIRS
    n_gate_dots = 2 * (D_MODEL // COL_CHUNK)
    blocks_per_round, rem = divmod(n_blocks, n_gate_dots)
    assert rem == 0 and blocks_per_round >= 1

    def wcols(w_ref, c0, c1):
        return jnp.concatenate([w_ref[t] for t in range(c0 // COL_CHUNK, c1 // COL_CHUNK)], axis=1)

    def block_slices(k):
        j, pair = divmod(k, N_PAIRS)
        return (slice(j * BLOCK, (j + 1) * BLOCK), slice(j * BLOCK, (j + 2) * BLOCK),
                slice(pair * LANES, (pair + 1) * LANES), j, pair)

    def input_norm(x_ref, slot):
        hn_scr[slot] = (_rms_scale(x_ref[...]) * a_pre_ref[...]).astype(bf16)

    def half_step(attn_set, seq_start_b, x_ref, conv_set, seq_start_a, out_rows, mid_hook=None):
        def b_scores(k):
            rows, key_rows, lanes, _, pair = block_slices(k)
            q_pair = q_scr[attn_set, pair, rows, :]
            for parity in range(HEADS_PER_VREG):
                s_scr[k % SCORE_SLOTS, parity] = jax.lax.dot_general(
                    q_pair,
                    kz_scr[attn_set, (pair // PAIRS_PER_KV) * HEADS_PER_VREG + parity, key_rows, :],
                    (((1,), (1,)), ((), ())), preferred_element_type=f32)

        sink_lane = jax.lax.broadcasted_iota(jnp.int32, (BLOCK, LANES), 1) == 0
        low_lanes = jax.lax.broadcasted_iota(jnp.int32, (BLOCK, LANES), 1) < HEAD_DIM

        def b_finish(k):
            rows, key_rows, lanes, j, pair = block_slices(k)
            first_block = jnp.where(seq_start_b, 1, 0) if (j == 0 and seq_start_b is not False) else 0
            slot0 = (pair // PAIRS_PER_KV) * HEADS_PER_VREG
            probs, inv_denoms = [], []
            for parity in range(HEADS_PER_VREG):
                head = pair * HEADS_PER_VREG + parity
                s = s_scr[k % SCORE_SLOTS, parity] + bias_scr[first_block, head]
                s = jnp.concatenate(
                    [jnp.where(sink_lane, sinks_ref[0, head] * LOG2_E, s[:, 0:LANES]), s[:, LANES:]], axis=1)
                m = jnp.max(s, axis=-1, keepdims=True)
                p = jnp.exp2(s - m)
                inv_denoms.append(1.0 / jnp.sum(p, axis=-1, keepdims=True))
                p = p.astype(bf16)
                probs.append(jnp.concatenate(
                    [jnp.where(sink_lane, jnp.zeros((), bf16), p[:, 0:LANES]), p[:, LANES:]], axis=1))
            values = jnp.concatenate(
                [vz_scr[attn_set, slot0 + parity, key_rows, :] for parity in range(HEADS_PER_VREG)],
                axis=0)
            pv = jnp.dot(jnp.concatenate(probs, axis=1), values, preferred_element_type=f32)
            o_pair = pv * jnp.where(low_lanes, inv_denoms[0], inv_denoms[1])
            g_scr[pair, rows, :] = (o_pair * z_scr[attn_set, pair, rows, :]).astype(bf16)

        def a_gate_dot(i):
            chunk, half = divmod(i, 2)
            cols = slice(chunk * COL_CHUNK, (chunk + 1) * COL_CHUNK)
            w0 = (chunk * 4 + 2 * half) * COL_CHUNK
            pair = jnp.dot(hn_scr[conv_set], wcols(w_in_ref, w0, w0 + 2 * COL_CHUNK),
                           preferred_element_type=f32)
            first, second = pair[:, 0:COL_CHUNK], pair[:, COL_CHUNK:2 * COL_CHUNK]
            if half == 0:
                v = first * second
                if seq_start_a is True:
                    prev = jnp.zeros((SUBLANES, COL_CHUNK), f32)
                elif seq_start_a is False:
                    prev = v_scr[chunk]
                else:
                    prev = jnp.where(seq_start_a, 0.0, v_scr[chunk])
                row = jax.lax.broadcasted_iota(jnp.int32, (SUBLANES, COL_CHUNK), 0)

                def delayed(k):
                    rolled = pltpu.roll(v, k, axis=0)
                    head = jnp.where(row < k, pltpu.roll(prev, k, axis=0), rolled[0:SUBLANES])
                    return jnp.concatenate([head, rolled[SUBLANES:]], axis=0)

                gate_scr[...] = (conv_w_ref[0, :, cols] * delayed(2)
                                 + conv_w_ref[1, :, cols] * delayed(1)
                                 + conv_w_ref[2, :, cols] * v)
                v_scr[chunk] = v[tm - SUBLANES:tm]
            else:
                y_scr[chunk] = (first * gate_scr[...]
                                  * (second * jax.nn.sigmoid(second))).astype(bf16)

        if attn_set is not None:
            for k in range(SCORE_SLOTS - 1):
                b_scores(k)
        for r in range(n_gate_dots):
            a_gate_dot(r)
            if attn_set is not None:
                for k in range(r * blocks_per_round, (r + 1) * blocks_per_round):
                    b_finish(k)
                    if k + SCORE_SLOTS - 1 < n_blocks:
                        b_scores(k + SCORE_SLOTS - 1)
            if mid_hook is not None and r == n_gate_dots // 2:
                mid_hook()

        y_in = jnp.concatenate([y_scr[c] for c in range(D_MODEL // COL_CHUNK)], axis=1)
        y = jnp.dot(y_in, wcols(w_out_a_ref, 0, D_MODEL), preferred_element_type=f32)
        if attn_set is not None:
            g_in = jnp.concatenate([g_scr[p] for p in range(N_PAIRS)], axis=1)
            y2 = jnp.dot(g_in, wcols(w_out_b_ref, 0, D_MODEL), preferred_element_type=f32)
        for scr in (kz_scr, vz_scr):
            if seq_start_a is True:
                scr[conv_set, :, 0:BLOCK, :] = jnp.zeros((2 * N_KV_HEADS, BLOCK, LANES), bf16)
            elif seq_start_a is False:
                scr[conv_set, :, 0:BLOCK, :] = scr[1 - conv_set, :, tm:tm + BLOCK, :]
            else:
                scr[conv_set, :, 0:BLOCK, :] = jnp.where(
                    seq_start_a, 0.0, scr[1 - conv_set, :, tm:tm + BLOCK, :].astype(f32)).astype(bf16)
        h1 = x_ref[...] + _rms_scale(y) * a_post_ref[...]
        h1_scr[conv_set] = h1
        h1n = _rms_scale(h1)
        kv = jnp.dot((h1n * kv_norm_ref[...]).astype(bf16), wcols(w_kv_ref, 0, 2 * KV_WIDTH),
                     preferred_element_type=f32)
        low_half = jax.lax.broadcasted_iota(jnp.int32, (tm, LANES), 1) < HEAD_DIM
        for src, dst in ((kv[:, 0:KV_WIDTH], kz_scr), (kv[:, KV_WIDTH:2 * KV_WIDTH], vz_scr)):
            swapped = pltpu.roll(src, HEAD_DIM, axis=1)
            dst[conv_set, 0, BLOCK:BLOCK + tm, :] = jnp.where(low_half, src, 0.0).astype(bf16)
            dst[conv_set, 1, BLOCK:BLOCK + tm, :] = jnp.where(low_half, 0.0, swapped).astype(bf16)
            dst[conv_set, 2, BLOCK:BLOCK + tm, :] = jnp.where(low_half, swapped, 0.0).astype(bf16)
            dst[conv_set, 3, BLOCK:BLOCK + tm, :] = jnp.where(low_half, 0.0, src).astype(bf16)
        qn = (h1n * b_pre_ref[...]).astype(bf16)
        q = jnp.dot(qn, wcols(w_in_b_ref, 0, D_MODEL), preferred_element_type=f32)
        for pair in range(N_PAIRS):
            q_scr[conv_set, pair] = (q[:, pair * LANES:(pair + 1) * LANES]
                                     * (HEAD_DIM ** -0.5 * LOG2_E)).astype(bf16)
        if attn_set is not None:
            out_ref[out_rows, :] = h1_scr[attn_set] + _rms_scale(y2) * b_post_ref[...]
        z = jnp.dot(qn, wcols(w_in_b_ref, D_MODEL, 2 * D_MODEL), preferred_element_type=f32)
        for pair in range(N_PAIRS):
            z_pair = z[:, pair * LANES:(pair + 1) * LANES]
            z_scr[conv_set, pair] = z_pair * jax.nn.sigmoid(z_pair)

    @pl.when(step == 0)
    def _init():
        bucket = bucket_ref[...]
        has_prev = jax.lax.broadcasted_iota(jnp.int32, bucket.shape, 1) >= BLOCK

        def build_bias(h):
            acc = jnp.full(bucket.shape, NEG_INF, f32)
            for b in range(N_BUCKETS):
                acc = jnp.where(bucket == b, rel_bias_ref[h, b] * LOG2_E, acc)
            bias_scr[0, h] = acc
            bias_scr[1, h] = jnp.where(has_prev, acc, NEG_INF)

        chunks = [(src, c0, min(WEIGHT_CHUNK, src.shape[1]), dst)
                  for src, dst in ((w_in_hbm, w_in_ref), (w_out_a_hbm, w_out_a_ref),
                                   (w_kv_hbm, w_kv_ref), (w_in_b_hbm, w_in_b_ref),
                                   (w_out_b_hbm, w_out_b_ref))
                  for c0 in range(0, src.shape[1], WEIGHT_CHUNK)]

        def chunk_copy(i):
            src, c0, width, _ = chunks[i]
            slot = i % STAGE_SLOTS
            return pltpu.make_async_copy(src.at[:, c0:c0 + width],
                                         stage_scr.at[slot, :, 0:width], stage_sem.at[slot])

        def gate_major_col(col):
            gate, rest = divmod(col, D_MODEL)
            chunk, lane = divmod(rest, COL_CHUNK)
            return (chunk * 4 + GATE_SLOT[gate]) * COL_CHUNK + lane

        for i in range(min(STAGE_SLOTS, len(chunks))):
            chunk_copy(i).start()
        for i, (_, c0, width, dst) in enumerate(chunks):
            if i < N_Q_HEADS:
                build_bias(i)
            chunk_copy(i).wait()
            for off in range(0, width, COL_CHUNK):
                d0 = gate_major_col(c0 + off) if dst is w_in_ref else c0 + off
                dst[d0 // COL_CHUNK] = stage_scr[i % STAGE_SLOTS, :, off:off + COL_CHUNK].astype(bf16)
            if i + STAGE_SLOTS < len(chunks):
                chunk_copy(i + STAGE_SLOTS).start()
        for h in range(len(chunks), N_Q_HEADS):
            build_bias(h)
        input_norm(x_first_ref, 0)
        half_step(None, False, x_first_ref, 0, True, None)

    even_seq_start = jax.lax.rem(TILES_PER_STEP * step + TILES_PER_STEP, tiles_per_seq) == 0
    attn_seq_start = jax.lax.rem(TILES_PER_STEP * step, tiles_per_seq) == 0
    input_norm(x_odd_ref, 1)
    half_step(0, attn_seq_start, x_odd_ref, 1, False, slice(0, tm),
              mid_hook=lambda: input_norm(x_even_ref, 0))
    half_step(1, False, x_even_ref, 0, even_seq_start, slice(tm, 2 * tm))


def _resident(shape):
    return pl.BlockSpec(shape, lambda s: (0,) * len(shape), pipeline_mode=pl.Buffered(1))


def _nbytes(shape, dtype):
    return math.prod(shape) * jnp.dtype(dtype).itemsize


@jax.jit
def kernel(x, a_pre_norm, a_w_in, a_conv_w, a_w_out, a_post_norm, kv_norm, w_kv, rel_bias,
           b_pre_norm, b_w_in, b_sinks, b_w_out, b_post_norm):
    bsz, seq, d_model = x.shape
    tm = ROW_TILE
    assert d_model == D_MODEL and seq % (tm * TILES_PER_STEP) == 0 and TILES_PER_STEP == 2
    assert a_w_in.shape[0] == 1 and b_w_in.shape[0] == 1
    tiles_per_seq = seq // tm
    n_tiles = bsz * tiles_per_seq
    bf16 = jnp.bfloat16
    f32 = jnp.float32
    smem = pl.BlockSpec(memory_space=pltpu.SMEM)
    hbm = pl.BlockSpec(memory_space=pl.ANY)
    weights = (a_w_in[0], a_w_out[0], w_kv, b_w_in[0], b_w_out[0])
    assert all(w.dtype == f32 and w.shape[0] == D_MODEL for w in weights)
    gains = [g.reshape(1, D_MODEL).astype(f32)
             for g in (a_pre_norm, a_post_norm, kv_norm, b_pre_norm, b_post_norm)]
    assert a_conv_w.shape == (1, CONV_K, D_MODEL)
    conv_taps = a_conv_w.transpose(1, 0, 2).astype(f32)
    bucket = _bucket_table()
    x2d = x.reshape(bsz * seq, D_MODEL)

    in_specs = [
        smem,
        smem,
        _resident(bucket.shape),
    ] + [_resident(g.shape) for g in gains] + [
        _resident(conv_taps.shape),
        _resident((tm, D_MODEL)),
        pl.BlockSpec((tm, D_MODEL), lambda s: (2 * s + 1, 0)),
        pl.BlockSpec((tm, D_MODEL), lambda s: (jnp.minimum(2 * s + 2, n_tiles - 1), 0)),
    ] + [hbm] * len(weights)
    per_set = 2
    vmem_scratch = [((w.shape[1] // COL_CHUNK, w.shape[0], COL_CHUNK), bf16) for w in weights] + [
        ((STAGE_SLOTS, D_MODEL, WEIGHT_CHUNK), f32),
        ((2, N_Q_HEADS, BLOCK, 2 * BLOCK), f32),
        ((per_set, tm, D_MODEL), bf16),
        ((D_MODEL // COL_CHUNK, SUBLANES, COL_CHUNK), f32),
        ((D_MODEL // COL_CHUNK, tm, COL_CHUNK), bf16),
        ((per_set, tm, D_MODEL), f32),
        ((per_set, N_PAIRS, tm, LANES), bf16),
        ((per_set, N_PAIRS, tm, LANES), f32),
        ((per_set, 2 * N_KV_HEADS, tm + BLOCK, LANES), bf16),
        ((per_set, 2 * N_KV_HEADS, tm + BLOCK, LANES), bf16),
        ((N_PAIRS, tm, LANES), bf16),
        ((SCORE_SLOTS, HEADS_PER_VREG, BLOCK, 2 * BLOCK), f32),
        ((tm, COL_CHUNK), f32),
    ]
    n_weights = len(weights)
    scratch_shapes = ([pltpu.VMEM(*sd) for sd in vmem_scratch[:n_weights + 1]]
                      + [pltpu.SemaphoreType.DMA((STAGE_SLOTS,))]
                      + [pltpu.VMEM(*sd) for sd in vmem_scratch[n_weights + 1:]])
    vmem_limit = (sum(_nbytes(*sd) for sd in vmem_scratch)
                  + PIPELINE_BUFFERS * (2 + TILES_PER_STEP) * _nbytes((tm, D_MODEL), f32)
                  + _nbytes((tm, D_MODEL), f32)
                  + _nbytes(bucket.shape, bucket.dtype)
                  + (len(gains) + CONV_K) * _nbytes((SUBLANES, D_MODEL), f32)
                  + SPILL_ALLOWANCE_BYTES)
    out = pl.pallas_call(
        functools.partial(_trunk_kernel, tiles_per_seq=tiles_per_seq),
        grid=(n_tiles // TILES_PER_STEP,),
        in_specs=in_specs,
        out_specs=pl.BlockSpec((TILES_PER_STEP * tm, D_MODEL), lambda s: (s, 0)),
        out_shape=jax.ShapeDtypeStruct((bsz * seq, D_MODEL), f32),
        scratch_shapes=scratch_shapes,
        compiler_params=pltpu.CompilerParams(
            dimension_semantics=("arbitrary",),
            vmem_limit_bytes=vmem_limit),
        name="yoco_trunk",
    )(rel_bias.T.astype(f32), b_sinks.reshape(1, N_Q_HEADS).astype(f32),
      jnp.asarray(bucket), *gains, conv_taps, x2d, x2d, x2d, *weights)
    return out.reshape(bsz, seq, D_MODEL)
```

```python
import functools
import math

import jax
import jax.numpy as jnp
import numpy as np
from jax.experimental import pallas as pl
from jax.experimental.pallas import tpu as pltpu

D_MODEL = 1024
CONV_K = 3
HEAD_DIM = 64
N_Q_HEADS = D_MODEL // HEAD_DIM
N_KV_HEADS = 2
GROUP = N_Q_HEADS // N_KV_HEADS
KV_WIDTH = N_KV_HEADS * HEAD_DIM
BLOCK = 128
N_BUCKETS = 32
MAX_DISTANCE = 128
EPS = 1e-6
NEG_INF = -1e30
LOG2_E = math.log2(math.e)

LANES = 128
SUBLANES = 8
HEADS_PER_VREG = LANES // HEAD_DIM
N_PAIRS = N_Q_HEADS // HEADS_PER_VREG
PAIRS_PER_KV = GROUP // HEADS_PER_VREG
ROW_TILE = 256
TILES_PER_STEP = 2
COL_CHUNK = 256
GATE_SLOT = (2, 0, 1, 3)
SCORE_SLOTS = 3
WEIGHT_CHUNK = 512
STAGE_SLOTS = 3
PIPELINE_BUFFERS = 2
SPILL_ALLOWANCE_BYTES = 12 * 1024 * 1024


def _bucket_table():
    q_loc = np.arange(BLOCK, dtype=np.int32)[:, None]
    s_loc = np.arange(2 * BLOCK, dtype=np.int32)[None, :]
    dist = q_loc + BLOCK - s_loc
    in_window = (dist >= 0) & (dist < BLOCK)
    d = np.maximum(dist, 0)
    max_exact = N_BUCKETS // 2
    df = np.maximum(d, 1).astype(np.float32)
    large = max_exact + (
        np.log(df / np.float32(max_exact)) / np.float32(math.log(MAX_DISTANCE / max_exact))
        * np.float32(N_BUCKETS - max_exact)).astype(np.int32)
    large = np.minimum(large, N_BUCKETS - 1)
    bucket = np.where(d < max_exact, d, large)
    return np.where(in_window, bucket, -1).astype(np.int32)


def _rms_scale(x):
    return x * jax.lax.rsqrt(jnp.mean(x * x, axis=-1, keepdims=True) + EPS)


def _trunk_kernel(rel_bias_ref, sinks_ref, bucket_ref, a_pre_ref, a_post_ref, kv_norm_ref, b_pre_ref,
                  b_post_ref, conv_w_ref, x_first_ref, x_odd_ref, x_even_ref,
                  w_in_hbm, w_out_a_hbm, w_kv_hbm, w_in_b_hbm, w_out_b_hbm,
                  out_hbm,
                  w_in_ref, w_out_a_ref, w_kv_ref, w_in_b_ref, w_out_b_ref, stage_scr, stage_sem,
                  bias_scr, hn_scr, v_scr, y_scr, h1_scr, q_scr, z_scr, kz_scr, vz_scr, g_scr,
                  s_scr, gate_scr, out_scr, out_sem,
                  *, tiles_per_seq):
    step = pl.program_id(0)
    tm = ROW_TILE
    f32 = jnp.float32
    bf16 = jnp.bfloat16

    n_blocks = (tm // BLOCK) * N_PAIRS
    n_gate_dots = 2 * (D_MODEL // COL_CHUNK)
    blocks_per_round, rem = divmod(n_blocks, n_gate_dots)
    assert rem == 0 and blocks_per_round >= 1

    def wcols(w_ref, c0, c1):
        return jnp.concatenate([w_ref[t] for t in range(c0 // COL_CHUNK, c1 // COL_CHUNK)], axis=1)

    def block_slices(k):
        j, pair = divmod(k, N_PAIRS)
        return (slice(j * BLOCK, (j + 1) * BLOCK), slice(j * BLOCK, (j + 2) * BLOCK),
                slice(pair * LANES, (pair + 1) * LANES), j, pair)

    def input_norm(x_ref, slot):
        hn_scr[slot] = (_rms_scale(x_ref[...]) * a_pre_ref[...]).astype(bf16)

    def out_copy(slot):
        row0 = pl.multiple_of((TILES_PER_STEP * step + slot) * tm, tm)
        return pltpu.make_async_copy(out_scr.at[slot], out_hbm.at[pl.ds(row0, tm), :],
                                     out_sem.at[slot])

    def write_out(slot, value):
        out_copy(slot).wait()
        out_scr[slot] = value
        out_copy(slot).start()

    def half_step(attn_set, seq_start_b, x_ref, conv_set, seq_start_a, out_slot, mid_hook=None):
        def b_scores(k):
            rows, key_rows, lanes, _, pair = block_slices(k)
            q_pair = q_scr[attn_set, pair, rows, :]
            for parity in range(HEADS_PER_VREG):
                s_scr[k % SCORE_SLOTS, parity] = jax.lax.dot_general(
                    q_pair,
                    kz_scr[attn_set, (pair // PAIRS_PER_KV) * HEADS_PER_VREG + parity, key_rows, :],
                    (((1,), (1,)), ((), ())), preferred_element_type=f32)

        sink_lane = jax.lax.broadcasted_iota(jnp.int32, (BLOCK, LANES), 1) == 0
        low_lanes = jax.lax.broadcasted_iota(jnp.int32, (BLOCK, LANES), 1) < HEAD_DIM

        def b_finish(k):
            rows, key_rows, lanes, j, pair = block_slices(k)
            first_block = jnp.where(seq_start_b, 1, 0) if (j == 0 and seq_start_b is not False) else 0
            slot0 = (pair // PAIRS_PER_KV) * HEADS_PER_VREG
            probs, inv_denoms = [], []
            for parity in range(HEADS_PER_VREG):
                head = pair * HEADS_PER_VREG + parity
                s = s_scr[k % SCORE_SLOTS, parity] + bias_scr[first_block, head]
                s = jnp.concatenate(
                    [jnp.where(sink_lane, sinks_ref[0, head] * LOG2_E, s[:, 0:LANES]), s[:, LANES:]], axis=1)
                m = jnp.max(s, axis=-1, keepdims=True)
                p = jnp.exp2(s - m)
                inv_denoms.append(1.0 / jnp.sum(p, axis=-1, keepdims=True))
                p = p.astype(bf16)
                probs.append(jnp.concatenate(
                    [jnp.where(sink_lane, jnp.zeros((), bf16), p[:, 0:LANES]), p[:, LANES:]], axis=1))
            values = jnp.concatenate(
                [vz_scr[attn_set, slot0 + parity, key_rows, :] for parity in range(HEADS_PER_VREG)],
                axis=0)
            pv = jnp.dot(jnp.concatenate(probs, axis=1), values, preferred_element_type=f32)
            o_pair = pv * jnp.where(low_lanes, inv_denoms[0], inv_denoms[1])
            g_scr[pair, rows, :] = (o_pair * z_scr[attn_set, pair, rows, :]).astype(bf16)

        def a_gate_dot(i):
            chunk, half = divmod(i, 2)
            cols = slice(chunk * COL_CHUNK, (chunk + 1) * COL_CHUNK)
            w0 = (chunk * 4 + 2 * half) * COL_CHUNK
            pair = jnp.dot(hn_scr[conv_set], wcols(w_in_ref, w0, w0 + 2 * COL_CHUNK),
                           preferred_element_type=f32)
            first, second = pair[:, 0:COL_CHUNK], pair[:, COL_CHUNK:2 * COL_CHUNK]
            if half == 0:
                v = first * second
                if seq_start_a is True:
                    prev = jnp.zeros((SUBLANES, COL_CHUNK), f32)
                elif seq_start_a is False:
                    prev = v_scr[chunk]
                else:
                    prev = jnp.where(seq_start_a, 0.0, v_scr[chunk])
                row = jax.lax.broadcasted_iota(jnp.int32, (SUBLANES, COL_CHUNK), 0)

                def delayed(k):
                    rolled = pltpu.roll(v, k, axis=0)
                    head = jnp.where(row < k, pltpu.roll(prev, k, axis=0), rolled[0:SUBLANES])
                    return jnp.concatenate([head, rolled[SUBLANES:]], axis=0)

                gate_scr[...] = (conv_w_ref[0, :, cols] * delayed(2)
                                 + conv_w_ref[1, :, cols] * delayed(1)
                                 + conv_w_ref[2, :, cols] * v)
                v_scr[chunk] = v[tm - SUBLANES:tm]
            else:
                y_scr[chunk] = (first * gate_scr[...]
                                  * (second * jax.nn.sigmoid(second))).astype(bf16)

        if attn_set is not None:
            for k in range(SCORE_SLOTS - 1):
                b_scores(k)
        for r in range(n_gate_dots):
            a_gate_dot(r)
            if attn_set is not None:
                for k in range(r * blocks_per_round, (r + 1) * blocks_per_round):
                    b_finish(k)
                    if k + SCORE_SLOTS - 1 < n_blocks:
                        b_scores(k + SCORE_SLOTS - 1)
            if mid_hook is not None and r == n_gate_dots // 2:
                mid_hook()

        y_in = jnp.concatenate([y_scr[c] for c in range(D_MODEL // COL_CHUNK)], axis=1)
        y = jnp.dot(y_in, wcols(w_out_a_ref, 0, D_MODEL), preferred_element_type=f32)
        if attn_set is not None:
            g_in = jnp.concatenate([g_scr[p] for p in range(N_PAIRS)], axis=1)
            y2 = jnp.dot(g_in, wcols(w_out_b_ref, 0, D_MODEL), preferred_element_type=f32)
        for scr in (kz_scr, vz_scr):
            if seq_start_a is True:
                scr[conv_set, :, 0:BLOCK, :] = jnp.zeros((2 * N_KV_HEADS, BLOCK, LANES), bf16)
            elif seq_start_a is False:
                scr[conv_set, :, 0:BLOCK, :] = scr[1 - conv_set, :, tm:tm + BLOCK, :]
            else:
                scr[conv_set, :, 0:BLOCK, :] = jnp.where(
                    seq_start_a, 0.0, scr[1 - conv_set, :, tm:tm + BLOCK, :].astype(f32)).astype(bf16)
        h1 = x_ref[...] + _rms_scale(y) * a_post_ref[...]
        h1_scr[conv_set] = h1
        h1n = _rms_scale(h1)
        kv = jnp.dot((h1n * kv_norm_ref[...]).astype(bf16), wcols(w_kv_ref, 0, 2 * KV_WIDTH),
                     preferred_element_type=f32)
        low_half = jax.lax.broadcasted_iota(jnp.int32, (tm, LANES), 1) < HEAD_DIM
        for src, dst in ((kv[:, 0:KV_WIDTH], kz_scr), (kv[:, KV_WIDTH:2 * KV_WIDTH], vz_scr)):
            swapped = pltpu.roll(src, HEAD_DIM, axis=1)
            dst[conv_set, 0, BLOCK:BLOCK + tm, :] = jnp.where(low_half, src, 0.0).astype(bf16)
            dst[conv_set, 1, BLOCK:BLOCK + tm, :] = jnp.where(low_half, 0.0, swapped).astype(bf16)
            dst[conv_set, 2, BLOCK:BLOCK + tm, :] = jnp.where(low_half, swapped, 0.0).astype(bf16)
            dst[conv_set, 3, BLOCK:BLOCK + tm, :] = jnp.where(low_half, 0.0, src).astype(bf16)
        qn = (h1n * b_pre_ref[...]).astype(bf16)
        q = jnp.dot(qn, wcols(w_in_b_ref, 0, D_MODEL), preferred_element_type=f32)
        for pair in range(N_PAIRS):
            q_scr[conv_set, pair] = (q[:, pair * LANES:(pair + 1) * LANES]
                                     * (HEAD_DIM ** -0.5 * LOG2_E)).astype(bf16)
        if attn_set is not None:
            write_out(out_slot, h1_scr[attn_set] + _rms_scale(y2) * b_post_ref[...])
        z = jnp.dot(qn, wcols(w_in_b_ref, D_MODEL, 2 * D_MODEL), preferred_element_type=f32)
        for pair in range(N_PAIRS):
            z_pair = z[:, pair * LANES:(pair + 1) * LANES]
            z_scr[conv_set, pair] = z_pair * jax.nn.sigmoid(z_pair)

    @pl.when(step == 0)
    def _init():
        out_scr[...] = jnp.zeros(out_scr.shape, f32)
        for slot in range(TILES_PER_STEP):
            out_copy(slot).start()
        bucket = bucket_ref[...]
        has_prev = jax.lax.broadcasted_iota(jnp.int32, bucket.shape, 1) >= BLOCK

        def build_bias(h):
            acc = jnp.full(bucket.shape, NEG_INF, f32)
            for b in range(N_BUCKETS):
                acc = jnp.where(bucket == b, rel_bias_ref[h, b] * LOG2_E, acc)
            bias_scr[0, h] = acc
            bias_scr[1, h] = jnp.where(has_prev, acc, NEG_INF)

        chunks = [(src, c0, min(WEIGHT_CHUNK, src.shape[1]), dst)
                  for src, dst in ((w_in_hbm, w_in_ref), (w_out_a_hbm, w_out_a_ref),
                                   (w_kv_hbm, w_kv_ref), (w_in_b_hbm, w_in_b_ref),
                                   (w_out_b_hbm, w_out_b_ref))
                  for c0 in range(0, src.shape[1], WEIGHT_CHUNK)]

        def chunk_copy(i):
            src, c0, width, _ = chunks[i]
            slot = i % STAGE_SLOTS
            return pltpu.make_async_copy(src.at[:, c0:c0 + width],
                                         stage_scr.at[slot, :, 0:width], stage_sem.at[slot])

        def gate_major_col(col):
            gate, rest = divmod(col, D_MODEL)
            chunk, lane = divmod(rest, COL_CHUNK)
            return (chunk * 4 + GATE_SLOT[gate]) * COL_CHUNK + lane

        for i in range(min(STAGE_SLOTS, len(chunks))):
            chunk_copy(i).start()
        for i, (_, c0, width, dst) in enumerate(chunks):
            if i < N_Q_HEADS:
                build_bias(i)
            chunk_copy(i).wait()
            for off in range(0, width, COL_CHUNK):
                d0 = gate_major_col(c0 + off) if dst is w_in_ref else c0 + off
                dst[d0 // COL_CHUNK] = stage_scr[i % STAGE_SLOTS, :, off:off + COL_CHUNK].astype(bf16)
            if i + STAGE_SLOTS < len(chunks):
                chunk_copy(i + STAGE_SLOTS).start()
        for h in range(len(chunks), N_Q_HEADS):
            build_bias(h)
        input_norm(x_first_ref, 0)
        half_step(None, False, x_first_ref, 0, True, None)

    even_seq_start = jax.lax.rem(TILES_PER_STEP * step + TILES_PER_STEP, tiles_per_seq) == 0
    attn_seq_start = jax.lax.rem(TILES_PER_STEP * step, tiles_per_seq) == 0
    input_norm(x_odd_ref, 1)
    half_step(0, attn_seq_start, x_odd_ref, 1, False, 0,
              mid_hook=lambda: input_norm(x_even_ref, 0))
    half_step(1, False, x_even_ref, 0, even_seq_start, 1)

    @pl.when(step == pl.num_programs(0) - 1)
    def _drain():
        for slot in range(TILES_PER_STEP):
            out_copy(slot).wait()


def _resident(shape):
    return pl.BlockSpec(shape, lambda s: (0,) * len(shape), pipeline_mode=pl.Buffered(1))


def _nbytes(shape, dtype):
    return math.prod(shape) * jnp.dtype(dtype).itemsize


@jax.jit
def kernel(x, a_pre_norm, a_w_in, a_conv_w, a_w_out, a_post_norm, kv_norm, w_kv, rel_bias,
           b_pre_norm, b_w_in, b_sinks, b_w_out, b_post_norm):
    bsz, seq, d_model = x.shape
    tm = ROW_TILE
    assert d_model == D_MODEL and seq % (tm * TILES_PER_STEP) == 0 and TILES_PER_STEP == 2
    assert a_w_in.shape[0] == 1 and b_w_in.shape[0] == 1
    tiles_per_seq = seq // tm
    n_tiles = bsz * tiles_per_seq
    bf16 = jnp.bfloat16
    f32 = jnp.float32
    smem = pl.BlockSpec(memory_space=pltpu.SMEM)
    hbm = pl.BlockSpec(memory_space=pl.ANY)
    weights = (a_w_in[0], a_w_out[0], w_kv, b_w_in[0], b_w_out[0])
    assert all(w.dtype == f32 and w.shape[0] == D_MODEL for w in weights)
    gains = [g.reshape(1, D_MODEL).astype(f32)
             for g in (a_pre_norm, a_post_norm, kv_norm, b_pre_norm, b_post_norm)]
    assert a_conv_w.shape == (1, CONV_K, D_MODEL)
    conv_taps = a_conv_w.transpose(1, 0, 2).astype(f32)
    bucket = _bucket_table()
    x2d = x.reshape(bsz * seq, D_MODEL)

    in_specs = [
        smem,
        smem,
        _resident(bucket.shape),
    ] + [_resident(g.shape) for g in gains] + [
        _resident(conv_taps.shape),
        _resident((tm, D_MODEL)),
        pl.BlockSpec((tm, D_MODEL), lambda s: (2 * s + 1, 0)),
        pl.BlockSpec((tm, D_MODEL), lambda s: (jnp.minimum(2 * s + 2, n_tiles - 1), 0)),
    ] + [hbm] * len(weights)
    per_set = 2
    vmem_scratch = [((w.shape[1] // COL_CHUNK, w.shape[0], COL_CHUNK), bf16) for w in weights] + [
        ((STAGE_SLOTS, D_MODEL, WEIGHT_CHUNK), f32),
        ((2, N_Q_HEADS, BLOCK, 2 * BLOCK), f32),
        ((per_set, tm, D_MODEL), bf16),
        ((D_MODEL // COL_CHUNK, SUBLANES, COL_CHUNK), f32),
        ((D_MODEL // COL_CHUNK, tm, COL_CHUNK), bf16),
        ((per_set, tm, D_MODEL), f32),
        ((per_set, N_PAIRS, tm, LANES), bf16),
        ((per_set, N_PAIRS, tm, LANES), f32),
        ((per_set, 2 * N_KV_HEADS, tm + BLOCK, LANES), bf16),
        ((per_set, 2 * N_KV_HEADS, tm + BLOCK, LANES), bf16),
        ((N_PAIRS, tm, LANES), bf16),
        ((SCORE_SLOTS, HEADS_PER_VREG, BLOCK, 2 * BLOCK), f32),
        ((tm, COL_CHUNK), f32),
        ((TILES_PER_STEP, tm, D_MODEL), f32),
    ]
    n_weights = len(weights)
    scratch_shapes = ([pltpu.VMEM(*sd) for sd in vmem_scratch[:n_weights + 1]]
                      + [pltpu.SemaphoreType.DMA((STAGE_SLOTS,))]
                      + [pltpu.VMEM(*sd) for sd in vmem_scratch[n_weights + 1:]]
                      + [pltpu.SemaphoreType.DMA((TILES_PER_STEP,))])
    vmem_limit = (sum(_nbytes(*sd) for sd in vmem_scratch)
                  + PIPELINE_BUFFERS * TILES_PER_STEP * _nbytes((tm, D_MODEL), f32)
                  + _nbytes((tm, D_MODEL), f32)
                  + _nbytes(bucket.shape, bucket.dtype)
                  + (len(gains) + CONV_K) * _nbytes((SUBLANES, D_MODEL), f32)
                  + SPILL_ALLOWANCE_BYTES)
    out = pl.pallas_call(
        functools.partial(_trunk_kernel, tiles_per_seq=tiles_per_seq),
        grid=(n_tiles // TILES_PER_STEP,),
        in_specs=in_specs,
        out_specs=hbm,
        out_shape=jax.ShapeDtypeStruct((bsz * seq, D_MODEL), f32),
        scratch_shapes=scratch_shapes,
        compiler_params=pltpu.CompilerParams(
            dimension_semantics=("arbitrary",),
            vmem_limit_bytes=vmem_limit),
        name="yoco_trunk",
    )(rel_bias.T.astype(f32), b_sinks.reshape(1, N_Q_HEADS).astype(f32),
      jnp.asarray(bucket), *gains, conv_taps, x2d, x2d, x2d, *weights)
    return out.reshape(bsz, seq, D_MODEL)
```

```python
import functools
import math

import jax
import jax.numpy as jnp
import numpy as np
from jax.experimental import pallas as pl
from jax.experimental.pallas import tpu as pltpu

D_MODEL = 1024
CONV_K = 3
HEAD_DIM = 64
N_Q_HEADS = D_MODEL // HEAD_DIM
N_KV_HEADS = 2
GROUP = N_Q_HEADS // N_KV_HEADS
KV_WIDTH = N_KV_HEADS * HEAD_DIM
BLOCK = 128
N_BUCKETS = 32
MAX_DISTANCE = 128
EPS = 1e-6
NEG_INF = -1e30
LOG2_E = math.log2(math.e)

LANES = 128
SUBLANES = 8
HEADS_PER_VREG = LANES // HEAD_DIM
N_PAIRS = N_Q_HEADS // HEADS_PER_VREG
PAIRS_PER_KV = GROUP // HEADS_PER_VREG
ROW_TILE = 256
TILES_PER_STEP = 2
COL_CHUNK = 256
GATE_SLOT = (2, 0, 1, 3)
SCORE_SLOTS = 3
WEIGHT_CHUNK = 512
STAGE_SLOTS = 3
PIPELINE_BUFFERS = 2
SPILL_ALLOWANCE_BYTES = 12 * 1024 * 1024


def _bucket_table():
    q_loc = np.arange(BLOCK, dtype=np.int32)[:, None]
    s_loc = np.arange(2 * BLOCK, dtype=np.int32)[None, :]
    dist = q_loc + BLOCK - s_loc
    in_window = (dist >= 0) & (dist < BLOCK)
    d = np.maximum(dist, 0)
    max_exact = N_BUCKETS // 2
    df = np.maximum(d, 1).astype(np.float32)
    large = max_exact + (
        np.log(df / np.float32(max_exact)) / np.float32(math.log(MAX_DISTANCE / max_exact))
        * np.float32(N_BUCKETS - max_exact)).astype(np.int32)
    large = np.minimum(large, N_BUCKETS - 1)
    bucket = np.where(d < max_exact, d, large)
    return np.where(in_window, bucket, -1).astype(np.int32)


def _rms_scale(x):
    return x * jax.lax.rsqrt(jnp.mean(x * x, axis=-1, keepdims=True) + EPS)


def _trunk_kernel(rel_bias_ref, sinks_ref, bucket_ref, a_pre_ref, a_post_ref, kv_norm_ref, b_pre_ref,
                  b_post_ref, conv_w_ref, x_first_ref, x_odd_ref, x_even_ref,
                  w_in_hbm, w_out_a_hbm, w_kv_hbm, w_in_b_hbm, w_out_b_hbm,
                  out_ref,
                  w_in_ref, w_out_a_ref, w_kv_ref, w_in_b_ref, w_out_b_ref, stage_scr, stage_sem,
                  bias_scr, hn_scr, v_scr, y_scr, h1_scr, q_scr, z_scr, kz_scr, vz_scr, g_scr,
                  s_scr, gate_scr,
                  *, tiles_per_seq):
    step = pl.program_id(0)
    tm = ROW_TILE
    f32 = jnp.float32
    bf16 = jnp.bfloat16

    n_blocks = (tm // BLOCK) * N_PAIRS
    n_gate_dots = 2 * (D_MODEL // COL_CHUNK)
    blocks_per_round, rem = divmod(n_blocks, n_gate_dots)
    assert rem == 0 and blocks_per_round >= 1

    def wcols(w_ref, c0, c1):
        return jnp.concatenate([w_ref[t] for t in range(c0 // COL_CHUNK, c1 // COL_CHUNK)], axis=1)

    def block_slices(k):
        j, pair = divmod(k, N_PAIRS)
        return (slice(j * BLOCK, (j + 1) * BLOCK), slice(j * BLOCK, (j + 2) * BLOCK),
                slice(pair * LANES, (pair + 1) * LANES), j, pair)

    def input_norm(x_ref, slot):
        hn_scr[slot] = (_rms_scale(x_ref[...]) * a_pre_ref[...]).astype(bf16)

    def half_step(attn_set, seq_start_b, x_ref, conv_set, seq_start_a, out_rows, mid_hook=None):
        def b_scores(k):
            rows, key_rows, lanes, _, pair = block_slices(k)
            q_pair = q_scr[attn_set, pair, rows, :]
            for parity in range(HEADS_PER_VREG):
                s_scr[k % SCORE_SLOTS, parity] = jax.lax.dot_general(
                    q_pair,
                    kz_scr[attn_set, (pair // PAIRS_PER_KV) * HEADS_PER_VREG + parity, key_rows, :],
                    (((1,), (1,)), ((), ())), preferred_element_type=f32)

        sink_lane = jax.lax.broadcasted_iota(jnp.int32, (BLOCK, LANES), 1) == 0
        low_lanes = jax.lax.broadcasted_iota(jnp.int32, (BLOCK, LANES), 1) < HEAD_DIM

        def b_finish(k):
            rows, key_rows, lanes, j, pair = block_slices(k)
            first_block = jnp.where(seq_start_b, 1, 0) if (j == 0 and seq_start_b is not False) else 0
            slot0 = (pair // PAIRS_PER_KV) * HEADS_PER_VREG
            probs, inv_denoms = [], []
            for parity in range(HEADS_PER_VREG):
                head = pair * HEADS_PER_VREG + parity
                s = s_scr[k % SCORE_SLOTS, parity] + bias_scr[first_block, head]
                s = jnp.concatenate(
                    [jnp.where(sink_lane, sinks_ref[0, head] * LOG2_E, s[:, 0:LANES]), s[:, LANES:]], axis=1)
                m = jnp.max(s, axis=-1, keepdims=True)
                p = jnp.exp2(s - m)
                inv_denoms.append(1.0 / jnp.sum(p, axis=-1, keepdims=True))
                p = p.astype(bf16)
                probs.append(jnp.concatenate(
                    [jnp.where(sink_lane, jnp.zeros((), bf16), p[:, 0:LANES]), p[:, LANES:]], axis=1))
            values = jnp.concatenate(
                [vz_scr[attn_set, slot0 + parity, key_rows, :] for parity in range(HEADS_PER_VREG)],
                axis=0)
            pv = jnp.dot(jnp.concatenate(probs, axis=1), values, preferred_element_type=f32)
            o_pair = pv * jnp.where(low_lanes, inv_denoms[0], inv_denoms[1])
            g_scr[pair, rows, :] = (o_pair * z_scr[attn_set, pair, rows, :]).astype(bf16)

        def a_gate_dot(i):
            chunk, half = divmod(i, 2)
            cols = slice(chunk * COL_CHUNK, (chunk + 1) * COL_CHUNK)
            w0 = (chunk * 4 + 2 * half) * COL_CHUNK
            pair = jnp.dot(hn_scr[conv_set], wcols(w_in_ref, w0, w0 + 2 * COL_CHUNK),
                           preferred_element_type=f32)
            first, second = pair[:, 0:COL_CHUNK], pair[:, COL_CHUNK:2 * COL_CHUNK]
            if half == 0:
                v = first * second
                if seq_start_a is True:
                    prev = jnp.zeros((SUBLANES, COL_CHUNK), f32)
                elif seq_start_a is False:
                    prev = v_scr[chunk]
                else:
                    prev = jnp.where(seq_start_a, 0.0, v_scr[chunk])
                row = jax.lax.broadcasted_iota(jnp.int32, (SUBLANES, COL_CHUNK), 0)

                def delayed(k):
                    rolled = pltpu.roll(v, k, axis=0)
                    head = jnp.where(row < k, pltpu.roll(prev, k, axis=0), rolled[0:SUBLANES])
                    return jnp.concatenate([head, rolled[SUBLANES:]], axis=0)

                gate_scr[...] = (conv_w_ref[0, :, cols] * delayed(2)
                                 + conv_w_ref[1, :, cols] * delayed(1)
                                 + conv_w_ref[2, :, cols] * v)
                v_scr[chunk] = v[tm - SUBLANES:tm]
            else:
                y_scr[chunk] = (first * gate_scr[...]
                                  * (second * jax.nn.sigmoid(second))).astype(bf16)

        if attn_set is not None:
            for k in range(SCORE_SLOTS - 1):
                b_scores(k)
        for r in range(n_gate_dots):
            a_gate_dot(r)
            if attn_set is not None:
                for k in range(r * blocks_per_round, (r + 1) * blocks_per_round):
                    b_finish(k)
                    if k + SCORE_SLOTS - 1 < n_blocks:
                        b_scores(k + SCORE_SLOTS - 1)
            if mid_hook is not None and r == n_gate_dots // 2:
                mid_hook()

        y_in = jnp.concatenate([y_scr[c] for c in range(D_MODEL // COL_CHUNK)], axis=1)
        y = jnp.dot(y_in, wcols(w_out_a_ref, 0, D_MODEL), preferred_element_type=f32)
        if attn_set is not None:
            g_in = jnp.concatenate([g_scr[p] for p in range(N_PAIRS)], axis=1)
            y2 = jnp.dot(g_in, wcols(w_out_b_ref, 0, D_MODEL), preferred_element_type=f32)
        for scr in (kz_scr, vz_scr):
            if seq_start_a is True:
                scr[conv_set, :, 0:BLOCK, :] = jnp.zeros((2 * N_KV_HEADS, BLOCK, LANES), bf16)
            elif seq_start_a is False:
                scr[conv_set, :, 0:BLOCK, :] = scr[1 - conv_set, :, tm:tm + BLOCK, :]
            else:
                scr[conv_set, :, 0:BLOCK, :] = jnp.where(
                    seq_start_a, 0.0, scr[1 - conv_set, :, tm:tm + BLOCK, :].astype(f32)).astype(bf16)
        h1 = x_ref[...] + _rms_scale(y) * a_post_ref[...]
        h1_scr[conv_set] = h1
        h1n = _rms_scale(h1)
        kv = jnp.dot((h1n * kv_norm_ref[...]).astype(bf16), wcols(w_kv_ref, 0, 2 * KV_WIDTH),
                     preferred_element_type=f32)
        low_half = jax.lax.broadcasted_iota(jnp.int32, (tm, LANES), 1) < HEAD_DIM
        for src, dst in ((kv[:, 0:KV_WIDTH], kz_scr), (kv[:, KV_WIDTH:2 * KV_WIDTH], vz_scr)):
            swapped = pltpu.roll(src, HEAD_DIM, axis=1)
            dst[conv_set, 0, BLOCK:BLOCK + tm, :] = jnp.where(low_half, src, 0.0).astype(bf16)
            dst[conv_set, 1, BLOCK:BLOCK + tm, :] = jnp.where(low_half, 0.0, swapped).astype(bf16)
            dst[conv_set, 2, BLOCK:BLOCK + tm, :] = jnp.where(low_half, swapped, 0.0).astype(bf16)
            dst[conv_set, 3, BLOCK:BLOCK + tm, :] = jnp.where(low_half, 0.0, src).astype(bf16)
        qn = (h1n * b_pre_ref[...]).astype(bf16)
        q = jnp.dot(qn, wcols(w_in_b_ref, 0, D_MODEL), preferred_element_type=f32)
        for pair in range(N_PAIRS):
            q_scr[conv_set, pair] = (q[:, pair * LANES:(pair + 1) * LANES]
                                     * (HEAD_DIM ** -0.5 * LOG2_E)).astype(bf16)
        if attn_set is not None:
            out_ref[out_rows, :] = h1_scr[attn_set] + _rms_scale(y2) * b_post_ref[...]
        z = jnp.dot(qn, wcols(w_in_b_ref, D_MODEL, 2 * D_MODEL), preferred_element_type=f32)
        for pair in range(N_PAIRS):
            z_pair = z[:, pair * LANES:(pair + 1) * LANES]
            z_scr[conv_set, pair] = z_pair * jax.nn.sigmoid(z_pair)

    @pl.when(step == 0)
    def _init():
        bucket = bucket_ref[...]
        has_prev = jax.lax.broadcasted_iota(jnp.int32, bucket.shape, 1) >= BLOCK

        def build_bias(h):
            acc = jnp.full(bucket.shape, NEG_INF, f32)
            for b in range(N_BUCKETS):
                acc = jnp.where(bucket == b, rel_bias_ref[h, b] * LOG2_E, acc)
            bias_scr[0, h] = acc
            bias_scr[1, h] = jnp.where(has_prev, acc, NEG_INF)

        chunks = [(src, c0, min(WEIGHT_CHUNK, src.shape[1]), dst)
                  for src, dst in ((w_in_hbm, w_in_ref), (w_out_a_hbm, w_out_a_ref),
                                   (w_kv_hbm, w_kv_ref), (w_in_b_hbm, w_in_b_ref),
                                   (w_out_b_hbm, w_out_b_ref))
                  for c0 in range(0, src.shape[1], WEIGHT_CHUNK)]

        def chunk_copy(i):
            src, c0, width, _ = chunks[i]
            slot = i % STAGE_SLOTS
            return pltpu.make_async_copy(src.at[:, c0:c0 + width],
                                         stage_scr.at[slot, :, 0:width], stage_sem.at[slot])

        def gate_major_col(col):
            gate, rest = divmod(col, D_MODEL)
            chunk, lane = divmod(rest, COL_CHUNK)
            return (chunk * 4 + GATE_SLOT[gate]) * COL_CHUNK + lane

        for i in range(min(STAGE_SLOTS, len(chunks))):
            chunk_copy(i).start(priority=i % 2)
        for i, (_, c0, width, dst) in enumerate(chunks):
            if i < N_Q_HEADS:
                build_bias(i)
            chunk_copy(i).wait()
            for off in range(0, width, COL_CHUNK):
                d0 = gate_major_col(c0 + off) if dst is w_in_ref else c0 + off
                dst[d0 // COL_CHUNK] = stage_scr[i % STAGE_SLOTS, :, off:off + COL_CHUNK].astype(bf16)
            if i + STAGE_SLOTS < len(chunks):
                chunk_copy(i + STAGE_SLOTS).start(priority=(i + STAGE_SLOTS) % 2)
        for h in range(len(chunks), N_Q_HEADS):
            build_bias(h)
        input_norm(x_first_ref, 0)
        half_step(None, False, x_first_ref, 0, True, None)

    even_seq_start = jax.lax.rem(TILES_PER_STEP * step + TILES_PER_STEP, tiles_per_seq) == 0
    attn_seq_start = jax.lax.rem(TILES_PER_STEP * step, tiles_per_seq) == 0
    input_norm(x_odd_ref, 1)
    half_step(0, attn_seq_start, x_odd_ref, 1, False, slice(0, tm),
              mid_hook=lambda: input_norm(x_even_ref, 0))
    half_step(1, False, x_even_ref, 0, even_seq_start, slice(tm, 2 * tm))


def _resident(shape):
    return pl.BlockSpec(shape, lambda s: (0,) * len(shape), pipeline_mode=pl.Buffered(1))


def _nbytes(shape, dtype):
    return math.prod(shape) * jnp.dtype(dtype).itemsize


@jax.jit
def kernel(x, a_pre_norm, a_w_in, a_conv_w, a_w_out, a_post_norm, kv_norm, w_kv, rel_bias,
           b_pre_norm, b_w_in, b_sinks, b_w_out, b_post_norm):
    bsz, seq, d_model = x.shape
    tm = ROW_TILE
    assert d_model == D_MODEL and seq % (tm * TILES_PER_STEP) == 0 and TILES_PER_STEP == 2
    assert a_w_in.shape[0] == 1 and b_w_in.shape[0] == 1
    tiles_per_seq = seq // tm
    n_tiles = bsz * tiles_per_seq
    bf16 = jnp.bfloat16
    f32 = jnp.float32
    smem = pl.BlockSpec(memory_space=pltpu.SMEM)
    hbm = pl.BlockSpec(memory_space=pl.ANY)
    weights = (a_w_in[0], a_w_out[0], w_kv, b_w_in[0], b_w_out[0])
    assert all(w.dtype == f32 and w.shape[0] == D_MODEL for w in weights)
    gains = [g.reshape(1, D_MODEL).astype(f32)
             for g in (a_pre_norm, a_post_norm, kv_norm, b_pre_norm, b_post_norm)]
    assert a_conv_w.shape == (1, CONV_K, D_MODEL)
    conv_taps = a_conv_w.transpose(1, 0, 2).astype(f32)
    bucket = _bucket_table()
    x2d = x.reshape(bsz * seq, D_MODEL)

    in_specs = [
        smem,
        smem,
        _resident(bucket.shape),
    ] + [_resident(g.shape) for g in gains] + [
        _resident(conv_taps.shape),
        _resident((tm, D_MODEL)),
        pl.BlockSpec((tm, D_MODEL), lambda s: (2 * s + 1, 0)),
        pl.BlockSpec((tm, D_MODEL), lambda s: (jnp.minimum(2 * s + 2, n_tiles - 1), 0)),
    ] + [hbm] * len(weights)
    per_set = 2
    vmem_scratch = [((w.shape[1] // COL_CHUNK, w.shape[0], COL_CHUNK), bf16) for w in weights] + [
        ((STAGE_SLOTS, D_MODEL, WEIGHT_CHUNK), f32),
        ((2, N_Q_HEADS, BLOCK, 2 * BLOCK), f32),
        ((per_set, tm, D_MODEL), bf16),
        ((D_MODEL // COL_CHUNK, SUBLANES, COL_CHUNK), f32),
        ((D_MODEL // COL_CHUNK, tm, COL_CHUNK), bf16),
        ((per_set, tm, D_MODEL), f32),
        ((per_set, N_PAIRS, tm, LANES), bf16),
        ((per_set, N_PAIRS, tm, LANES), f32),
        ((per_set, 2 * N_KV_HEADS, tm + BLOCK, LANES), bf16),
        ((per_set, 2 * N_KV_HEADS, tm + BLOCK, LANES), bf16),
        ((N_PAIRS, tm, LANES), bf16),
        ((SCORE_SLOTS, HEADS_PER_VREG, BLOCK, 2 * BLOCK), f32),
        ((tm, COL_CHUNK), f32),
    ]
    n_weights = len(weights)
    scratch_shapes = ([pltpu.VMEM(*sd) for sd in vmem_scratch[:n_weights + 1]]
                      + [pltpu.SemaphoreType.DMA((STAGE_SLOTS,))]
                      + [pltpu.VMEM(*sd) for sd in vmem_scratch[n_weights + 1:]])
    vmem_limit = (sum(_nbytes(*sd) for sd in vmem_scratch)
                  + PIPELINE_BUFFERS * (2 + TILES_PER_STEP) * _nbytes((tm, D_MODEL), f32)
                  + _nbytes((tm, D_MODEL), f32)
                  + _nbytes(bucket.shape, bucket.dtype)
                  + (len(gains) + CONV_K) * _nbytes((SUBLANES, D_MODEL), f32)
                  + SPILL_ALLOWANCE_BYTES)
    out = pl.pallas_call(
        functools.partial(_trunk_kernel, tiles_per_seq=tiles_per_seq),
        grid=(n_tiles // TILES_PER_STEP,),
        in_specs=in_specs,
        out_specs=pl.BlockSpec((TILES_PER_STEP * tm, D_MODEL), lambda s: (s, 0)),
        out_shape=jax.ShapeDtypeStruct((bsz * seq, D_MODEL), f32),
        scratch_shapes=scratch_shapes,
        compiler_params=pltpu.CompilerParams(
            dimension_semantics=("arbitrary",),
            vmem_limit_bytes=vmem_limit),
        name="yoco_trunk",
    )(rel_bias.T.astype(f32), b_sinks.reshape(1, N_Q_HEADS).astype(f32),
      jnp.asarray(bucket), *gains, conv_taps, x2d, x2d, x2d, *weights)
    return out.reshape(bsz, seq, D_MODEL)
```
